```python
import jax, jax.numpy as jnp
from jax import lax
import numpy as np

D_MODEL = 2048
BATCH = 1
SEQ = 8192
DEPTH = 2
DEC_BATCH = 32
DEC_SEQ = 4
PAST_LEN = 8192
PAGE_SIZE = 128

N_MIXERS = 2
N_REC_LAYERS = (DEPTH + 1) // 2
N_ATT_LAYERS = DEPTH // 2
D_RNN = D_MODEL
N_RG_BLOCKS = 8
RG_BLOCK = D_RNN // N_RG_BLOCKS
CONV_W = 4
RG_C = 8.0
N_HEADS = 16
HEAD_DIM = D_MODEL // N_HEADS
Q_BLOCK = 128
ATT_SCALE = HEAD_DIM ** -0.5
PEER_HEADS = 8
N_KEYS = 128
N_EXPERTS = N_KEYS * N_KEYS
D_KEY = 256
HALF_KEY = D_KEY // 2
TOPK = 16
PEER_BLOCK = 128

EPS = 1e-6
NEG_INF = -1e30

kernel_name = "hybrid_rglru_fox_peer_step"


def rms_norm(x, g):
    xf = x.astype(jnp.float32)
    y = xf * lax.rsqrt(jnp.mean(xf * xf, axis=-1, keepdims=True) + EPS)
    return (y * g.astype(jnp.float32)).astype(x.dtype)


def _lin_combine(left, right):
    a1, b1 = left
    a2, b2 = right
    return a1 * a2, a2 * b1 + b2


def rg_lru_mixer(xn, conv_buf, h0, w_in, conv_w, conv_b, w_a, b_a, w_i, b_i, lam, w_out):
    B, T, _ = xn.shape
    proj = xn @ w_in
    gate, xr = proj[..., :D_RNN], proj[..., D_RNN:]
    x_ext = jnp.concatenate([conv_buf.astype(xr.dtype), xr], axis=1)
    xc = conv_b + sum(x_ext[:, k:k + T] * conv_w[k] for k in range(CONV_W))
    new_buf = x_ext[:, T:]
    xb = xc.reshape(B, T, N_RG_BLOCKS, RG_BLOCK)
    r = jax.nn.sigmoid((jnp.einsum('btnc,ncd->btnd', xb, w_a).reshape(B, T, D_RNN) + b_a).astype(jnp.float32))
    i = jax.nn.sigmoid((jnp.einsum('btnc,ncd->btnd', xb, w_i).reshape(B, T, D_RNN) + b_i).astype(jnp.float32))
    log_a = RG_C * r * jax.nn.log_sigmoid(lam.astype(jnp.float32))
    a = jnp.exp(log_a)
    b = jnp.sqrt(-jnp.expm1(2.0 * log_a)) * (i * xc.astype(jnp.float32))
    b = b.at[:, 0].add(a[:, 0] * h0.astype(jnp.float32))
    _, h = lax.associative_scan(_lin_combine, (a, b), axis=1)
    y = (h * jax.nn.gelu(gate.astype(jnp.float32))).astype(xn.dtype) @ w_out
    return y, new_buf, h[:, -1]


def fox_project(xn, w_in, b_f, q_g, k_g):
    B, T, _ = xn.shape
    proj = xn @ w_in
    q = proj[..., :D_MODEL].reshape(B, T, N_HEADS, HEAD_DIM)
    k = proj[..., D_MODEL:2 * D_MODEL].reshape(B, T, N_HEADS, HEAD_DIM)
    v = proj[..., 2 * D_MODEL:3 * D_MODEL].reshape(B, T, N_HEADS, HEAD_DIM)
    logf = jax.nn.log_sigmoid((proj[..., 3 * D_MODEL:] + b_f).astype(jnp.float32))
    return rms_norm(q, q_g), rms_norm(k, k_g), v, logf


def fox_prompt_attention(q, k, v, logf):
    B, S, H, Dh = q.shape
    nb = S // Q_BLOCK
    c = jnp.cumsum(logf, axis=1)
    c_keys = c.transpose(0, 2, 1)
    qb = q.reshape(B, nb, Q_BLOCK, H, Dh).swapaxes(0, 1)
    cb = c.reshape(B, nb, Q_BLOCK, H).swapaxes(0, 1)
    k_pos = jnp.arange(S)

    def block(args):
        qi, ci, bi = args
        logits = jnp.einsum('bthd,bshd->bhts', qi, k).astype(jnp.float32) * ATT_SCALE
        logits = logits + ci.transpose(0, 2, 1)[..., None] - c_keys[:, :, None, :]
        q_pos = bi * Q_BLOCK + jnp.arange(Q_BLOCK)
        mask = k_pos[None, :] <= q_pos[:, None]
        p = jax.nn.softmax(jnp.where(mask, logits, NEG_INF), axis=-1).astype(v.dtype)
        return jnp.einsum('bhts,bshd->bthd', p, v)

    out = lax.map(block, (qb, cb, jnp.arange(nb)))
    return out.swapaxes(0, 1).reshape(B, S, H * Dh)


def fox_sample_attention(q, k, v, logf, k_cache, v_cache, logf_cache, page_table):
    Bd, T, H, Dh = q.shape
    past = page_table.shape[1] * PAGE_SIZE
    kp = k_cache[page_table].reshape(Bd, past, H, Dh)
    vp = v_cache[page_table].reshape(Bd, past, H, Dh)
    lp = logf_cache[page_table].reshape(Bd, past, H).astype(jnp.float32)
    c_past = jnp.cumsum(lp, axis=1)
    c_new = c_past[:, -1:] + jnp.cumsum(logf, axis=1)
    cq = c_new.transpose(0, 2, 1)
    lg_past = (jnp.einsum('bthd,bshd->bhts', q, kp).astype(jnp.float32) * ATT_SCALE
               + cq[..., None] - c_past.transpose(0, 2, 1)[:, :, None, :])
    lg_new = (jnp.einsum('bthd,bshd->bhts', q, k).astype(jnp.float32) * ATT_SCALE
              + cq[..., None] - cq[:, :, None, :])
    causal = jnp.tril(jnp.ones((T, T), dtype=bool))
    lg_new = jnp.where(causal, lg_new, NEG_INF)
    p = jax.nn.softmax(jnp.concatenate([lg_past, lg_new], axis=-1), axis=-1).astype(v.dtype)
    out = (jnp.einsum('bhts,bshd->bthd', p[..., :past], vp)
           + jnp.einsum('bhts,bshd->bthd', p[..., past:], v))
    return out.reshape(Bd, T, H * Dh)


def peer_ffn(xn, w_q, sub_keys, u, v):
    shp = xn.shape
    x2 = xn.reshape(-1, D_MODEL)
    n = x2.shape[0]
    nb = -(-n // PEER_BLOCK)
    xp = jnp.pad(x2, ((0, nb * PEER_BLOCK - n), (0, 0))).reshape(nb, PEER_BLOCK, D_MODEL)

    def block(xt):
        q = (xt @ w_q).reshape(PEER_BLOCK, PEER_HEADS, 2, HALF_KEY).astype(jnp.float32)
        s = jnp.einsum('nhpc,hpkc->nhpk', q, sub_keys.astype(jnp.float32))
        sv, si = lax.top_k(s, TOPK)
        cand = sv[:, :, 0, :, None] + sv[:, :, 1, None, :]
        cidx = si[:, :, 0, :, None] * N_KEYS + si[:, :, 1, None, :]
        cand = cand.reshape(PEER_BLOCK, PEER_HEADS, TOPK * TOPK)
        cidx = cidx.reshape(PEER_BLOCK, PEER_HEADS, TOPK * TOPK)
        top_s, pos = lax.top_k(cand, TOPK)
        eidx = jnp.take_along_axis(cidx, pos, axis=-1)
        g = jax.nn.softmax(top_s, axis=-1)
        ue = u[eidx]
        ve = v[eidx]
        act = jax.nn.gelu(jnp.einsum('nd,nhkd->nhk', xt, ue).astype(jnp.float32))
        return jnp.einsum('nhk,nhkd->nd', (g * act).astype(xt.dtype), ve)

    out = lax.map(block, xp)
    return out.reshape(-1, D_MODEL)[:n].reshape(shp)


def setup_inputs(seed: int = 0) -> dict:
    key = jax.random.key(seed)
    ks = jax.random.split(key, 32)
    f32 = jnp.float32
    n_pages = PAST_LEN // PAGE_SIZE
    n_pool = (DEC_BATCH * n_pages * 5) // 4

    def nrm(k, shape, scale):
        return jax.random.normal(k, shape, f32) * scale

    x_prompt = nrm(ks[0], (BATCH, SEQ, D_MODEL), 1.0)
    x_sample = nrm(ks[1], (DEC_BATCH, DEC_SEQ, D_MODEL), 1.0)
    state_conv = nrm(ks[2], (N_REC_LAYERS, DEC_BATCH, CONV_W - 1, D_RNN), 1.0)
    state_h = nrm(ks[3], (N_REC_LAYERS, DEC_BATCH, D_RNN), 0.5)
    cache_k = nrm(ks[4], (N_ATT_LAYERS, n_pool, PAGE_SIZE, N_HEADS, HEAD_DIM), 1.0)
    cache_v = nrm(ks[5], (N_ATT_LAYERS, n_pool, PAGE_SIZE, N_HEADS, HEAD_DIM), 1.0)
    cache_logf = jax.nn.log_sigmoid(3.0 + nrm(ks[6], (N_ATT_LAYERS, n_pool, PAGE_SIZE, N_HEADS), 1.0))
    page_table = jax.random.permutation(ks[7], n_pool)[:DEC_BATCH * n_pages].reshape(DEC_BATCH, n_pages).astype(jnp.int32)
    norm_mix = 1.0 + nrm(ks[8], (DEPTH, D_MODEL), 0.02)
    norm_ffn = 1.0 + nrm(ks[9], (DEPTH, D_MODEL), 0.02)
    rec_w_in = nrm(ks[10], (N_REC_LAYERS, D_MODEL, 2 * D_RNN), D_MODEL ** -0.5)
    rec_conv_w = nrm(ks[11], (N_REC_LAYERS, CONV_W, D_RNN), CONV_W ** -0.5)
    rec_conv_b = nrm(ks[12], (N_REC_LAYERS, D_RNN), 0.01)
    rec_w_a = nrm(ks[13], (N_REC_LAYERS, N_RG_BLOCKS, RG_BLOCK, RG_BLOCK), RG_BLOCK ** -0.5)
    rec_b_a = nrm(ks[14], (N_REC_LAYERS, D_RNN), 0.01)
    rec_w_i = nrm(ks[15], (N_REC_LAYERS, N_RG_BLOCKS, RG_BLOCK, RG_BLOCK), RG_BLOCK ** -0.5)
    rec_b_i = nrm(ks[16], (N_REC_LAYERS, D_RNN), 0.01)
    a0 = jax.random.uniform(ks[17], (N_REC_LAYERS, D_RNN), f32, 0.9, 0.999)
    rec_lambda = jnp.log(a0) - jnp.log1p(-a0)
    rec_w_out = nrm(ks[18], (N_REC_LAYERS, D_RNN, D_MODEL), D_RNN ** -0.5)
    att_w_in = nrm(ks[19], (N_ATT_LAYERS, D_MODEL, 3 * D_MODEL + N_HEADS), D_MODEL ** -0.5)
    att_b_f = 1.0 + 5.0 * jax.random.uniform(ks[20], (N_ATT_LAYERS, N_HEADS), f32)
    att_q_norm = 1.0 + nrm(ks[21], (N_ATT_LAYERS, HEAD_DIM), 0.02)
    att_k_norm = 1.0 + nrm(ks[22], (N_ATT_LAYERS, HEAD_DIM), 0.02)
    att_w_out = nrm(ks[23], (N_ATT_LAYERS, D_MODEL, D_MODEL), D_MODEL ** -0.5)
    peer_w_q = nrm(ks[24], (DEPTH, D_MODEL, PEER_HEADS * D_KEY), D_MODEL ** -0.5)
    peer_sub_keys = nrm(ks[25], (DEPTH, PEER_HEADS, 2, N_KEYS, HALF_KEY), HALF_KEY ** -0.5)
    peer_u = nrm(ks[26], (DEPTH, N_EXPERTS, D_MODEL), D_MODEL ** -0.5)
    peer_v = nrm(ks[27], (DEPTH, N_EXPERTS, D_MODEL), PEER_HEADS ** -0.5)
    return {"x_prompt": x_prompt, "x_sample": x_sample, "state_conv": state_conv, "state_h": state_h,
            "cache_k": cache_k, "cache_v": cache_v, "cache_logf": cache_logf, "page_table": page_table,
            "norm_mix": norm_mix, "norm_ffn": norm_ffn,
            "rec_w_in": rec_w_in, "rec_conv_w": rec_conv_w, "rec_conv_b": rec_conv_b,
            "rec_w_a": rec_w_a, "rec_b_a": rec_b_a, "rec_w_i": rec_w_i, "rec_b_i": rec_b_i,
            "rec_lambda": rec_lambda, "rec_w_out": rec_w_out,
            "att_w_in": att_w_in, "att_b_f": att_b_f, "att_q_norm": att_q_norm, "att_k_norm": att_k_norm,
            "att_w_out": att_w_out,
            "peer_w_q": peer_w_q, "peer_sub_keys": peer_sub_keys, "peer_u": peer_u, "peer_v": peer_v}


def reference(x_prompt, x_sample, state_conv, state_h, cache_k, cache_v, cache_logf, page_table,
              norm_mix, norm_ffn,
              rec_w_in, rec_conv_w, rec_conv_b, rec_w_a, rec_b_a, rec_w_i, rec_b_i, rec_lambda, rec_w_out,
              att_w_in, att_b_f, att_q_norm, att_k_norm, att_w_out,
              peer_w_q, peer_sub_keys, peer_u, peer_v):
    bp = x_prompt.shape[0]
    zero_buf = jnp.zeros((bp, CONV_W - 1, D_RNN), x_prompt.dtype)
    zero_h = jnp.zeros((bp, D_RNN), jnp.float32)
    xp, xs = x_prompt, x_sample
    pc, ph, pk, pv, pl = [], [], [], [], []
    sc, sh, sk, sv, sl = [], [], [], [], []
    for i in range(DEPTH):
        j = i // N_MIXERS
        hp = rms_norm(xp, norm_mix[i])
        hs = rms_norm(xs, norm_mix[i])
        if i % N_MIXERS == 0:
            rec_args = (rec_w_in[j], rec_conv_w[j], rec_conv_b[j], rec_w_a[j], rec_b_a[j],
                        rec_w_i[j], rec_b_i[j], rec_lambda[j], rec_w_out[j])
            yp, buf_p, hl_p = rg_lru_mixer(hp, zero_buf, zero_h, *rec_args)
            ys, buf_s, hl_s = rg_lru_mixer(hs, state_conv[j], state_h[j], *rec_args)
            pc.append(buf_p); ph.append(hl_p); sc.append(buf_s); sh.append(hl_s)
        else:
            proj_args = (att_w_in[j], att_b_f[j], att_q_norm[j], att_k_norm[j])
            q_p, k_p, v_p, lf_p = fox_project(hp, *proj_args)
            yp = fox_prompt_attention(q_p, k_p, v_p, lf_p) @ att_w_out[j]
            q_s, k_s, v_s, lf_s = fox_project(hs, *proj_args)
            ys = fox_sample_attention(q_s, k_s, v_s, lf_s, cache_k[j], cache_v[j], cache_logf[j], page_table) @ att_w_out[j]
            pk.append(k_p); pv.append(v_p); pl.append(lf_p)
            sk.append(k_s); sv.append(v_s); sl.append(lf_s)
        xp = xp + yp
        xs = xs + ys
        peer_args = (peer_w_q[i], peer_sub_keys[i], peer_u[i], peer_v[i])
        xp = xp + peer_ffn(rms_norm(xp, norm_ffn[i]), *peer_args)
        xs = xs + peer_ffn(rms_norm(xs, norm_ffn[i]), *peer_args)
    new_conv_prompt = jnp.stack(pc)
    new_h_prompt = jnp.stack(ph)
    new_k_prompt = jnp.stack(pk)
    new_v_prompt = jnp.stack(pv)
    new_logf_prompt = jnp.stack(pl)
    new_conv_sample = jnp.stack(sc)
    new_h_sample = jnp.stack(sh)
    new_k_sample = jnp.stack(sk)
    new_v_sample = jnp.stack(sv)
    new_logf_sample = jnp.stack(sl)
    return (xp, xs, new_conv_prompt, new_h_prompt, new_k_prompt, new_v_prompt, new_logf_prompt,
            new_conv_sample, new_h_sample, new_k_sample, new_v_sample, new_logf_sample)
```

```python
import functools
import math

import numpy as np
import jax
import jax.numpy as jnp
from jax import lax
from jax.experimental import pallas as pl
from jax.experimental.pallas import tpu as pltpu

F32 = jnp.float32
BF16 = jnp.bfloat16

EPS = 1e-6
NEG_INF = -1e30
RG_C = 8.0
CONV_W = 4
HEAD_DIM = 128
PAGE_SIZE = 128
TOPK = 16
N_KEYS = 128
LANE = 128
SUBLANE = 8
VMEM_LIMIT = 56 * 1024 * 1024


def _cparams(sem):
    return pltpu.CompilerParams(dimension_semantics=sem, vmem_limit_bytes=VMEM_LIMIT)


def _gelu(x):
    c = math.sqrt(2.0 / math.pi)
    return 0.5 * x * (1.0 + jnp.tanh(c * (x + 0.044715 * (x * x * x))))


def _log_sigmoid(z):
    return jnp.minimum(z, 0.0) - jnp.log1p(jnp.exp(-jnp.abs(z)))


def _sigmoid(z):
    return 1.0 / (1.0 + jnp.exp(-z))


def _dot_nt(a, b):
    return lax.dot_general(a, b, (((1,), (1,)), ((), ())), preferred_element_type=F32)


def _dot_tn(a, b):
    return lax.dot_general(a, b, (((0,), (0,)), ((), ())), preferred_element_type=F32)


def _mm_kernel(*refs, norm, residual):
    it = iter(refs)
    x_ref = next(it)
    g_ref = next(it) if norm else None
    w_ref = next(it)
    res_ref = next(it) if residual else None
    out_ref = next(it)
    xn_ref = next(it) if norm else None

    if norm:
        @pl.when(pl.program_id(1) == 0)
        def _():
            xf = x_ref[...].astype(F32)
            y = xf * lax.rsqrt(jnp.mean(xf * xf, axis=-1, keepdims=True) + EPS)
            xn_ref[...] = (y * g_ref[...]).astype(BF16)
        lhs = xn_ref[...]
    else:
        lhs = x_ref[...].astype(BF16)
    acc = jnp.dot(lhs, w_ref[...], preferred_element_type=F32)
    if residual:
        acc = acc + res_ref[...]
    out_ref[...] = acc.astype(out_ref.dtype)


def _mm(x, w, *, gain=None, residual=None, out_dtype=F32, tm=512, tn=512):
    M, K = x.shape
    N = w.shape[1]
    tm = min(tm, M)
    tn = min(tn, N)
    assert M % tm == 0 and N % tn == 0
    norm = gain is not None
    in_specs = [pl.BlockSpec((tm, K), lambda i, j: (i, 0))]
    args = [x]
    if norm:
        in_specs.append(pl.BlockSpec((1, K), lambda i, j: (0, 0)))
        args.append(gain.reshape(1, K).astype(F32))
    in_specs.append(pl.BlockSpec((K, tn), lambda i, j: (0, j)))
    args.append(w)
    if residual is not None:
        in_specs.append(pl.BlockSpec((tm, tn), lambda i, j: (i, j)))
        args.append(residual)
    scratch = [pltpu.VMEM((tm, K), BF16)] if norm else []
    return pl.pallas_call(
        functools.partial(_mm_kernel, norm=norm, residual=residual is not None),
        grid=(M // tm, N // tn),
        in_specs=in_specs,
        out_specs=pl.BlockSpec((tm, tn), lambda i, j: (i, j)),
        out_shape=jax.ShapeDtypeStruct((M, N), out_dtype),
        scratch_shapes=scratch,
        compiler_params=_cparams(("parallel", "arbitrary")),
        name="mm",
    )(*args)


def _rg_coeffs(xc, w_a, w_i, b_a, b_i, lam):
    xcb = xc.astype(BF16)
    r = _sigmoid(jnp.dot(xcb, w_a, preferred_element_type=F32) + b_a)
    i = _sigmoid(jnp.dot(xcb, w_i, preferred_element_type=F32) + b_i)
    log_a = RG_C * r * _log_sigmoid(lam)
    a = jnp.exp(log_a)
    th = jnp.tanh(log_a)
    b = jnp.sqrt(-2.0 * th / (1.0 - th)) * (i * xc)
    return a, b


def _scan_rows(a, b):
    n = a.shape[0]
    row = lax.broadcasted_iota(jnp.int32, a.shape, 0)
    s = 1
    while s < n:
        a_sh = pltpu.roll(a, s, axis=0)
        b_sh = pltpu.roll(b, s, axis=0)
        valid = row >= s
        b = jnp.where(valid, a * b_sh + b, b)
        a = jnp.where(valid, a * a_sh, a)
        s *= 2
    return a, b


def _rglru_prompt_kernel(gate_ref, xr_ref, cw_ref, cb_ref, wa_ref, wi_ref, ba_ref, bi_ref,
                         lam_ref, init_ref, h0_ref, y_ref, hl_ref, ext_ref, h_ref, *, tt):
    t = pl.program_id(1)

    @pl.when(t == 0)
    def _():
        ext_ref[0:SUBLANE, :] = init_ref[...]
        h_ref[...] = h0_ref[...]

    xr = xr_ref[...]
    ext_ref[SUBLANE:SUBLANE + tt, :] = xr
    xc = cb_ref[...] + xr * cw_ref[CONV_W - 1:CONV_W, :]
    for k in range(CONV_W - 1):
        xc = xc + ext_ref[pl.ds(SUBLANE - (CONV_W - 1) + k, tt), :] * cw_ref[k:k + 1, :]
    ext_ref[0:SUBLANE, :] = xr[tt - SUBLANE:tt, :]

    a, b = _rg_coeffs(xc, wa_ref[...], wi_ref[...], ba_ref[...], bi_ref[...], lam_ref[...])
    A, B = _scan_rows(a, b)
    h = A * h_ref[...] + B
    h_ref[...] = h[tt - 1:tt, :]
    hl_ref[...] = h[tt - 1:tt, :]
    y_ref[...] = (h * _gelu(gate_ref[...])).astype(y_ref.dtype)


def _rglru_prompt(proj, conv_init, h0, cw, cb, wa, wi, ba, bi, lam, *, tt=256):
    T = proj.shape[0]
    D = proj.shape[1] // 2
    nb, C = wa.shape[0], wa.shape[1]
    tt = min(tt, T)
    assert T % tt == 0 and tt % SUBLANE == 0
    vec = lambda: pl.BlockSpec((1, C), lambda n, t: (0, n))
    y, hl = pl.pallas_call(
        functools.partial(_rglru_prompt_kernel, tt=tt),
        grid=(nb, T // tt),
        in_specs=[
            pl.BlockSpec((tt, C), lambda n, t: (t, n)),
            pl.BlockSpec((tt, C), lambda n, t: (t, nb + n)),
            pl.BlockSpec((CONV_W, C), lambda n, t: (0, n)),
            vec(),
            pl.BlockSpec((None, C, C), lambda n, t: (n, 0, 0)),
            pl.BlockSpec((None, C, C), lambda n, t: (n, 0, 0)),
            vec(), vec(), vec(),
            pl.BlockSpec((SUBLANE, C), lambda n, t: (0, n)),
            vec(),
        ],
        out_specs=[pl.BlockSpec((tt, C), lambda n, t: (t, n)),
                   pl.BlockSpec((1, C), lambda n, t: (0, n))],
        out_shape=[jax.ShapeDtypeStruct((T, D), BF16), jax.ShapeDtypeStruct((1, D), F32)],
        scratch_shapes=[pltpu.VMEM((tt + SUBLANE, C), F32), pltpu.VMEM((1, C), F32)],
        compiler_params=_cparams(("parallel", "arbitrary")),
        name="rglru_prompt",
    )(proj, proj, cw, cb, wa, wi, ba, bi, lam, conv_init, h0)
    return y, hl


def _rglru_sample_kernel(gate_ref, ext_ref, cw_ref, cb_ref, wa_ref, wi_ref, ba_ref, bi_ref,
                         lam_ref, h0_ref, y_ref, hl_ref, *, steps):
    h = h0_ref[...]
    for t in range(steps):
        xc = cb_ref[...]
        for k in range(CONV_W):
            xc = xc + ext_ref[t + k] * cw_ref[k:k + 1, :]
        a, b = _rg_coeffs(xc, wa_ref[...], wi_ref[...], ba_ref[...], bi_ref[...], lam_ref[...])
        h = a * h + b
        y_ref[t] = (h * _gelu(gate_ref[t])).astype(y_ref.dtype)
    hl_ref[...] = h


def _rglru_sample(gate_tm, ext_tm, h0, cw, cb, wa, wi, ba, bi, lam):
    S, B, D = gate_tm.shape
    nb, C = wa.shape[0], wa.shape[1]
    vec = lambda: pl.BlockSpec((1, C), lambda n: (0, n))
    return pl.pallas_call(
        functools.partial(_rglru_sample_kernel, steps=S),
        grid=(nb,),
        in_specs=[
            pl.BlockSpec((S, B, C), lambda n: (0, 0, n)),
            pl.BlockSpec((S + CONV_W - 1, B, C), lambda n: (0, 0, n)),
            pl.BlockSpec((CONV_W, C), lambda n: (0, n)),
            vec(),
            pl.BlockSpec((None, C, C), lambda n: (n, 0, 0)),
            pl.BlockSpec((None, C, C), lambda n: (n, 0, 0)),
            vec(), vec(), vec(),
            pl.BlockSpec((B, C), lambda n: (0, n)),
        ],
        out_specs=[pl.BlockSpec((S, B, C), lambda n: (0, 0, n)),
                   pl.BlockSpec((B, C), lambda n: (0, n))],
        out_shape=[jax.ShapeDtypeStruct((S, B, D), BF16), jax.ShapeDtypeStruct((B, D), F32)],
        compiler_params=_cparams(("parallel",)),
        name="rglru_sample",
    )(gate_tm, ext_tm, cw, cb, wa, wi, ba, bi, lam, h0)


def _qkv_post_kernel(q_ref, k_ref, v_ref, f_ref, bf_ref, qg_ref, kg_ref,
                     qb_ref, kf_ref, kb_ref, vb_ref, lf_ref, nc_ref, carry_ref, *, scale):
    @pl.when(pl.program_id(0) == 0)
    def _():
        carry_ref[...] = jnp.zeros_like(carry_ref)

    n_heads = q_ref.shape[1] // HEAD_DIM
    for h in range(n_heads):
        sl = slice(h * HEAD_DIM, (h + 1) * HEAD_DIM)
        qh = q_ref[:, sl]
        qn = qh * lax.rsqrt(jnp.mean(qh * qh, axis=-1, keepdims=True) + EPS) * qg_ref[...]
        qb_ref[:, sl] = (qn * scale).astype(BF16)
        kh = k_ref[:, sl]
        kn = kh * lax.rsqrt(jnp.mean(kh * kh, axis=-1, keepdims=True) + EPS) * kg_ref[...]
        kf_ref[:, sl] = kn
        kb_ref[:, sl] = kn.astype(BF16)
    vb_ref[...] = v_ref[...].astype(BF16)
    lf = _log_sigmoid(f_ref[...] + bf_ref[...])
    lf_ref[...] = lf
    ones = jnp.ones_like(lf)
    _, c = _scan_rows(ones, lf)
    c = c + carry_ref[...]
    carry_ref[...] = c[c.shape[0] - 1:, :]
    nc_ref[...] = -c


def _qkv_post(proj, projf, b_f, q_g, k_g, *, tm=256):
    T = proj.shape[0]
    D = proj.shape[1] // 3
    tm = min(tm, T)
    assert T % tm == 0
    blk = lambda j: pl.BlockSpec((tm, D), lambda i: (i, j))
    small = pl.BlockSpec((tm, LANE), lambda i: (i, 0))
    vec = pl.BlockSpec((1, LANE), lambda i: (0, 0))
    return pl.pallas_call(
        functools.partial(_qkv_post_kernel, scale=HEAD_DIM ** -0.5),
        grid=(T // tm,),
        in_specs=[blk(0), blk(1), blk(2), small, vec, vec, vec],
        out_specs=[blk(0), blk(0), blk(0), blk(0), small, small],
        out_shape=[jax.ShapeDtypeStruct((T, D), BF16), jax.ShapeDtypeStruct((T, D), F32),
                   jax.ShapeDtypeStruct((T, D), BF16), jax.ShapeDtypeStruct((T, D), BF16),
                   jax.ShapeDtypeStruct((T, LANE), F32), jax.ShapeDtypeStruct((T, LANE), F32)],
        scratch_shapes=[pltpu.VMEM((1, LANE), F32)],
        compiler_params=_cparams(("arbitrary",)),
        name="qkv_post",
    )(proj, proj, proj, projf, b_f, q_g, k_g)


def _flash_kernel(qi_ref, ki_ref, q_ref, k_ref, v_ref, nc_ref, o_ref, m_ref, l_ref, acc_ref, *, tq):
    s_idx = pl.program_id(1)
    qi = qi_ref[s_idx]
    ki = ki_ref[s_idx]

    @pl.when(ki == 0)
    def _():
        m_ref[...] = jnp.full_like(m_ref, NEG_INF)
        l_ref[...] = jnp.zeros_like(l_ref)
        acc_ref[...] = jnp.zeros_like(acc_ref)

    def update(s):
        m_prev = m_ref[...]
        m_new = jnp.maximum(m_prev, jnp.max(s, axis=-1, keepdims=True))
        alpha = jnp.exp(m_prev - m_new)
        p = jnp.exp(s - m_new)
        l_ref[...] = alpha * l_ref[...] + jnp.sum(p, axis=-1, keepdims=True)
        acc_ref[...] = alpha * acc_ref[...] + jnp.dot(p.astype(BF16), v_ref[...],
                                                     preferred_element_type=F32)
        m_ref[...] = m_new

    s = _dot_nt(q_ref[...], k_ref[...]) + nc_ref[...]

    @pl.when(ki < qi)
    def _():
        update(s)

    @pl.when(ki == qi)
    def _():
        row = lax.broadcasted_iota(jnp.int32, s.shape, 0)
        col = lax.broadcasted_iota(jnp.int32, s.shape, 1)
        update(jnp.where(col <= row, s, NEG_INF))
        o_ref[...] = (acc_ref[...] / l_ref[...]).astype(o_ref.dtype)


def _flash(q, k, v, negc_t, *, tq=512):
    T, D = q.shape
    H = D // HEAD_DIM
    tq = min(tq, T)
    assert T % tq == 0
    nq = T // tq
    pairs = [(a, b) for a in range(nq) for b in range(a + 1)]
    qi_tab = jnp.asarray(np.array([p[0] for p in pairs], np.int32))
    ki_tab = jnp.asarray(np.array([p[1] for p in pairs], np.int32))
    grid_spec = pltpu.PrefetchScalarGridSpec(
        num_scalar_prefetch=2,
        grid=(H, len(pairs)),
        in_specs=[
            pl.BlockSpec((tq, HEAD_DIM), lambda h, s, qi, ki: (qi[s], h)),
            pl.BlockSpec((tq, HEAD_DIM), lambda h, s, qi, ki: (ki[s], h)),
            pl.BlockSpec((tq, HEAD_DIM), lambda h, s, qi, ki: (ki[s], h)),
            pl.BlockSpec((None, 1, tq), lambda h, s, qi, ki: (h, 0, ki[s])),
        ],
        out_specs=pl.BlockSpec((tq, HEAD_DIM), lambda h, s, qi, ki: (qi[s], h)),
        scratch_shapes=[pltpu.VMEM((tq, 1), F32), pltpu.VMEM((tq, 1), F32),
                        pltpu.VMEM((tq, HEAD_DIM), F32)],
    )
    return pl.pallas_call(
        functools.partial(_flash_kernel, tq=tq),
        grid_spec=grid_spec,
        out_shape=jax.ShapeDtypeStruct((T, D), BF16),
        compiler_params=_cparams(("parallel", "arbitrary")),
        name="flash",
    )(qi_tab, ki_tab, q, k, v, negc_t)


def _page_cumsum_kernel(pt_ref, cache_ref, new_ref, out_ref, carry_ref, *, n_pages):
    p = pl.program_id(1)

    @pl.when(p == 0)
    def _():
        carry_ref[...] = jnp.zeros_like(carry_ref)

    lf = jnp.where(p == n_pages, new_ref[...], cache_ref[...])
    _, c = _scan_rows(jnp.ones_like(lf), lf)
    c = c + carry_ref[...]
    carry_ref[...] = c[PAGE_SIZE - 1:, :]
    out_ref[...] = -c


def _page_cumsum(page_table, cache_logf, new_logf):
    Bd, P = page_table.shape
    H = cache_logf.shape[-1]
    last = P - 1
    grid_spec = pltpu.PrefetchScalarGridSpec(
        num_scalar_prefetch=1,
        grid=(Bd, P + 1),
        in_specs=[
            pl.BlockSpec((None, PAGE_SIZE, H), lambda b, p, pt: (pt[b, jnp.minimum(p, last)], 0, 0)),
            pl.BlockSpec((None, PAGE_SIZE, H), lambda b, p, pt: (b, 0, 0)),
        ],
        out_specs=pl.BlockSpec((None, PAGE_SIZE, H), lambda b, p, pt: (b, p, 0)),
        scratch_shapes=[pltpu.VMEM((1, H), F32)],
    )
    return pl.pallas_call(
        functools.partial(_page_cumsum_kernel, n_pages=P),
        grid_spec=grid_spec,
        out_shape=jax.ShapeDtypeStruct((Bd, (P + 1) * PAGE_SIZE, H), F32),
        compiler_params=_cparams(("parallel", "arbitrary")),
        name="page_cumsum",
    )(page_table, cache_logf, new_logf)


def _decode_kernel(pt_ref, q_ref, kc_ref, vc_ref, kn_ref, vn_ref, nc_ref, o_ref,
                   qbd_ref, m_ref, l_ref, acc_ref, *, n_pages, n_heads):
    p = pl.program_id(1)
    rows = q_ref.shape[0]

    def head_mask(shape):
        row = lax.broadcasted_iota(jnp.int32, shape, 0)
        col = lax.broadcasted_iota(jnp.int32, shape, 1)
        return (col // HEAD_DIM) == (row % n_heads)

    @pl.when(p == 0)
    def _():
        m_ref[...] = jnp.full_like(m_ref, NEG_INF)
        l_ref[...] = jnp.zeros_like(l_ref)
        acc_ref[...] = jnp.zeros_like(acc_ref)
        q = q_ref[...]
        qbd_ref[...] = jnp.where(head_mask(q.shape), q, jnp.zeros_like(q))

    nc = nc_ref[...]
    bias = jnp.concatenate([nc] * (rows // n_heads), axis=0)

    def update(kb, vb, causal):
        s = _dot_nt(qbd_ref[...], kb) + bias
        if causal:
            row = lax.broadcasted_iota(jnp.int32, s.shape, 0)
            col = lax.broadcasted_iota(jnp.int32, s.shape, 1)
            s = jnp.where(col <= row // n_heads, s, NEG_INF)
        m_prev = m_ref[...]
        m_new = jnp.maximum(m_prev, jnp.max(s, axis=-1, keepdims=True))
        alpha = jnp.exp(m_prev - m_new)
        pr = jnp.exp(s - m_new)
        l_ref[...] = alpha * l_ref[...] + jnp.sum(pr, axis=-1, keepdims=True)
        acc_ref[...] = alpha * acc_ref[...] + jnp.dot(pr.astype(BF16), vb, preferred_element_type=F32)
        m_ref[...] = m_new

    @pl.when(p < n_pages)
    def _():
        update(kc_ref[...].astype(BF16), vc_ref[...].astype(BF16), False)

    @pl.when(p == n_pages)
    def _():
        update(kn_ref[...], vn_ref[...], True)
        acc = acc_ref[...]
        acc = jnp.where(head_mask(acc.shape), acc, 0.0)
        out = acc[:, 0:HEAD_DIM]
        for h in range(1, n_heads):
            out = out + acc[:, h * HEAD_DIM:(h + 1) * HEAD_DIM]
        o_ref[...] = out / l_ref[...]


def _decode(page_table, q_rep, cache_k, cache_v, k_new, v_new, negc_t):
    Bd, P = page_table.shape
    R, D = q_rep.shape[1], q_rep.shape[2]
    H = D // HEAD_DIM
    last = P - 1
    grid_spec = pltpu.PrefetchScalarGridSpec(
        num_scalar_prefetch=1,
        grid=(Bd, P + 1),
        in_specs=[
            pl.BlockSpec((None, R, D), lambda b, p, pt: (b, 0, 0)),
            pl.BlockSpec((None, PAGE_SIZE, D), lambda b, p, pt: (pt[b, jnp.minimum(p, last)], 0, 0)),
            pl.BlockSpec((None, PAGE_SIZE, D), lambda b, p, pt: (pt[b, jnp.minimum(p, last)], 0, 0)),
            pl.BlockSpec((None, PAGE_SIZE, D), lambda b, p, pt: (b, 0, 0)),
            pl.BlockSpec((None, PAGE_SIZE, D), lambda b, p, pt: (b, 0, 0)),
            pl.BlockSpec((None, H, PAGE_SIZE), lambda b, p, pt: (b, 0, p)),
        ],
        out_specs=pl.BlockSpec((None, R, HEAD_DIM), lambda b, p, pt: (b, 0, 0)),
        scratch_shapes=[pltpu.VMEM((R, D), BF16), pltpu.VMEM((R, 1), F32), pltpu.VMEM((R, 1), F32),
                        pltpu.VMEM((R, D), F32)],
    )
    return pl.pallas_call(
        functools.partial(_decode_kernel, n_pages=P, n_heads=H),
        grid_spec=grid_spec,
        out_shape=jax.ShapeDtypeStruct((Bd, R, HEAD_DIM), F32),
        compiler_params=_cparams(("parallel", "arbitrary")),
        name="decode",
    )(page_table, q_rep, cache_k, cache_v, k_new, v_new, negc_t)


def _cmpx(v, i, j):
    hi = jnp.maximum(v[i], v[j])
    lo = jnp.minimum(v[i], v[j])
    v[i], v[j] = hi, lo


def _bitonic_merge_desc(v):
    n = len(v)
    j = n // 2
    while j >= 1:
        for i in range(n):
            if i & j == 0:
                _cmpx(v, i, i | j)
        j //= 2
    return v


def _sort_desc(v):
    v = list(v)
    n = len(v)
    k = 2
    while k <= n:
        j = k // 2
        while j >= 1:
            for i in range(n):
                l = i ^ j
                if l > i:
                    if (i & k) == 0:
                        _cmpx(v, i, l)
                    else:
                        _cmpx(v, l, i)
            j //= 2
        k *= 2
    return v


def _top16_sorted(v):
    groups = [_sort_desc(v[g:g + TOPK]) for g in range(0, len(v), TOPK)]
    while len(groups) > 1:
        nxt = []
        for g in range(0, len(groups), 2):
            a, b = groups[g], groups[g + 1]
            c = [jnp.maximum(a[i], b[TOPK - 1 - i]) for i in range(TOPK)]
            nxt.append(_bitonic_merge_desc(c))
        groups = nxt
    return groups[0]


def _peer_route_kernel(q_ref, kb_ref, s1_ref, s2_ref, e1_ref, e2_ref, tau_ref, *, n_heads):
    nrow = n_heads * N_KEYS
    tb = q_ref.shape[0]
    tops = []
    per_head = []
    for p in range(2):
        qp = q_ref[:, p * nrow:(p + 1) * nrow]
        res = _dot_nt(kb_ref[p], qp)
        srt = res[0:nrow].reshape(N_KEYS, n_heads, tb)
        per_head.append(res[nrow:2 * nrow].reshape(n_heads, N_KEYS, tb))
        tops.append(_top16_sorted([srt[k] for k in range(N_KEYS)]))
    v1, v2 = tops
    cands = [v1[a] + v2[b] for a in range(TOPK) for b in range(TOPK) if (a + 1) * (b + 1) <= TOPK]
    pad = -3.0e38 * jnp.ones_like(cands[0])
    cands = cands + [pad] * (4 * TOPK - len(cands))
    top = _top16_sorted(cands)
    z = jnp.ones_like(top[0])
    for k in range(1, TOPK):
        z = z + jnp.exp(top[k] - top[0])
    inv_z = 1.0 / z
    tau_ref[...] = top[TOPK - 1]
    s1, s2 = per_head
    s1_ref[...] = s1
    s2_ref[...] = s2
    for h in range(n_heads):
        e1_ref[h] = jnp.exp(s1[h] - v1[0][h:h + 1, :])
        e2_ref[h] = jnp.exp(s2[h] - v2[0][h:h + 1, :]) * inv_z[h:h + 1, :]


def _peer_route(q, kb, *, n_heads, tb=128):
    N = q.shape[0]
    tb = min(tb, N)
    assert N % tb == 0
    big = lambda: pl.BlockSpec((n_heads, N_KEYS, tb), lambda i: (0, 0, i))
    shp = jax.ShapeDtypeStruct((n_heads, N_KEYS, N), F32)
    return pl.pallas_call(
        functools.partial(_peer_route_kernel, n_heads=n_heads),
        grid=(N // tb,),
        in_specs=[pl.BlockSpec((tb, q.shape[1]), lambda i: (i, 0)),
                  pl.BlockSpec(kb.shape, lambda i: (0, 0, 0))],
        out_specs=[big(), big(), big(), big(), pl.BlockSpec((n_heads, tb), lambda i: (0, i))],
        out_shape=[shp, shp, shp, shp, jax.ShapeDtypeStruct((n_heads, N), F32)],
        compiler_params=_cparams(("parallel",)),
        name="peer_route",
    )(q, kb)


def _peer_dense_kernel(xn_ref, u_ref, v_ref, s1_ref, e1_ref, s2_ref, e2_ref, tau_ref, res_ref,
                       out_ref, wt_ref, *, n_heads, rows_per_step):
    e = pl.program_id(1)

    @pl.when(e == 0)
    def _():
        out_ref[...] = res_ref[...]

    act = _dot_nt(u_ref[...], xn_ref[...])
    wt_ref[...] = _gelu(act)

    def body(r, carry):
        rows = pl.ds(pl.multiple_of(r * N_KEYS, N_KEYS), N_KEYS)
        acc = jnp.zeros((N_KEYS, wt_ref.shape[1]), F32)
        for h in range(n_heads):
            s = s1_ref[h, pl.ds(r, 1), :] + s2_ref[h]
            sel = jnp.where(s >= tau_ref[h:h + 1, :], e2_ref[h], 0.0)
            acc = acc + sel * e1_ref[h, pl.ds(r, 1), :]
        wt_ref[rows, :] = wt_ref[rows, :] * acc
        return carry

    lax.fori_loop(0, rows_per_step, body, 0)
    out_ref[...] += _dot_tn(wt_ref[...].astype(BF16), v_ref[...])


def _peer_dense(xn, u, v, s1, s2, e1, e2, tau, resid, *, tb=512, te=1024):
    N, D = xn.shape
    E = u.shape[0]
    n_heads = s1.shape[0]
    tb = min(tb, N)
    assert N % tb == 0 and E % te == 0 and te % N_KEYS == 0
    rps = te // N_KEYS
    return pl.pallas_call(
        functools.partial(_peer_dense_kernel, n_heads=n_heads, rows_per_step=rps),
        grid=(N // tb, E // te),
        in_specs=[
            pl.BlockSpec((tb, D), lambda i, e: (i, 0)),
            pl.BlockSpec((te, D), lambda i, e: (e, 0)),
            pl.BlockSpec((te, D), lambda i, e: (e, 0)),
            pl.BlockSpec((n_heads, rps, tb), lambda i, e: (0, e, i)),
            pl.BlockSpec((n_heads, rps, tb), lambda i, e: (0, e, i)),
            pl.BlockSpec((n_heads, N_KEYS, tb), lambda i, e: (0, 0, i)),
            pl.BlockSpec((n_heads, N_KEYS, tb), lambda i, e: (0, 0, i)),
            pl.BlockSpec((n_heads, tb), lambda i, e: (0, i)),
            pl.BlockSpec((tb, D), lambda i, e: (i, 0)),
        ],
        out_specs=pl.BlockSpec((tb, D), lambda i, e: (i, 0)),
        out_shape=jax.ShapeDtypeStruct((N, D), F32),
        scratch_shapes=[pltpu.VMEM((te, tb), F32)],
        compiler_params=_cparams(("parallel", "arbitrary")),
        name="peer_dense",
    )(xn, u, v, s1, e1, s2, e2, tau, resid)


def _peer_weights(w_q, sub_keys):
    D = w_q.shape[0]
    H = sub_keys.shape[0]
    half = sub_keys.shape[-1]
    wq = w_q.reshape(D, H, 2, half).transpose(0, 2, 1, 3).reshape(D, 2 * H * half).astype(BF16)
    eye = jnp.eye(H, dtype=sub_keys.dtype)
    sk = sub_keys.transpose(1, 0, 2, 3)
    bd = sk[:, :, :, None, :] * eye[None, :, None, :, None]
    hk = bd.reshape(2, H * N_KEYS, H * half)
    kh = bd.transpose(0, 2, 1, 3, 4).reshape(2, N_KEYS * H, H * half)
    return wq, jnp.concatenate([kh, hk], axis=1).astype(BF16)


def _peer(x, gain, wq, kb, u, v, *, n_heads):
    N, D = x.shape
    xn = _mm_norm_only(x, gain)
    q = _mm(xn, wq, out_dtype=BF16)
    s1, s2, e1, e2, tau = _peer_route(q, kb, n_heads=n_heads)
    return _peer_dense(xn, u, v, s1, s2, e1, e2, tau, x)


def _norm_kernel(x_ref, g_ref, o_ref):
    xf = x_ref[...].astype(F32)
    y = xf * lax.rsqrt(jnp.mean(xf * xf, axis=-1, keepdims=True) + EPS)
    o_ref[...] = (y * g_ref[...]).astype(o_ref.dtype)


def _mm_norm_only(x, gain, *, tm=512):
    M, K = x.shape
    tm = min(tm, M)
    return pl.pallas_call(
        _norm_kernel,
        grid=(M // tm,),
        in_specs=[pl.BlockSpec((tm, K), lambda i: (i, 0)), pl.BlockSpec((1, K), lambda i: (0, 0))],
        out_specs=pl.BlockSpec((tm, K), lambda i: (i, 0)),
        out_shape=jax.ShapeDtypeStruct((M, K), BF16),
        compiler_params=_cparams(("parallel",)),
        name="rms_norm",
    )(x, gain.reshape(1, K).astype(F32))


def _pad_lanes(a, width=LANE):
    return jnp.pad(a, [(0, 0)] * (a.ndim - 1) + [(0, width - a.shape[-1])])


def kernel(x_prompt, x_sample, state_conv, state_h, cache_k, cache_v, cache_logf, page_table,
           norm_mix, norm_ffn,
           rec_w_in, rec_conv_w, rec_conv_b, rec_w_a, rec_b_a, rec_w_i, rec_b_i, rec_lambda, rec_w_out,
           att_w_in, att_b_f, att_q_norm, att_k_norm, att_w_out,
           peer_w_q, peer_sub_keys, peer_u, peer_v):
    Bp, S, D = x_prompt.shape
    Bd, Sd, _ = x_sample.shape
    assert Bp == 1
    H = D // HEAD_DIM
    PH = peer_sub_keys.shape[1]
    depth = norm_mix.shape[0]
    n_pool = cache_k.shape[1]
    n_pages = page_table.shape[1]

    xp = x_prompt.reshape(S, D)
    xs = x_sample.reshape(Bd * Sd, D)
    row = lambda a: a.reshape(1, -1).astype(F32)

    pc, ph, pk, pv, plf = [], [], [], [], []
    sc, sh, sk, sv, slf = [], [], [], [], []
    for i in range(depth):
        j = i // 2
        if i % 2 == 0:
            w_in = rec_w_in[j].astype(BF16)
            w_out = rec_w_out[j].astype(BF16)
            wa = rec_w_a[j].astype(BF16)
            wi = rec_w_i[j].astype(BF16)
            cw, cb = rec_conv_w[j], row(rec_conv_b[j])
            ba, bi, lam = row(rec_b_a[j]), row(rec_b_i[j]), row(rec_lambda[j])
            proj_p = _mm(xp, w_in, gain=norm_mix[i])
            yp, hl_p = _rglru_prompt(proj_p, jnp.zeros((SUBLANE, D), F32), jnp.zeros((1, D), F32),
                                     cw, cb, wa, wi, ba, bi, lam)
            pc.append(proj_p[S - (CONV_W - 1):, D:].reshape(1, CONV_W - 1, D))
            ph.append(hl_p)
            xp = _mm(yp, w_out, residual=xp)
            proj_s = _mm(xs, w_in, gain=norm_mix[i]).reshape(Bd, Sd, 2 * D)
            ext = jnp.concatenate([state_conv[j], proj_s[..., D:]], axis=1)
            ys_tm, hl_s = _rglru_sample(proj_s[..., :D].transpose(1, 0, 2), ext.transpose(1, 0, 2),
                                        state_h[j], cw, cb, wa, wi, ba, bi, lam)
            sc.append(ext[:, Sd:])
            sh.append(hl_s)
            xs = _mm(ys_tm.transpose(1, 0, 2).reshape(Bd * Sd, D), w_out, residual=xs)
        else:
            w_qkv = att_w_in[j][:, :3 * D].astype(BF16)
            w_f = _pad_lanes(att_w_in[j][:, 3 * D:]).astype(BF16)
            w_out = att_w_out[j].astype(BF16)
            b_f = _pad_lanes(row(att_b_f[j]))
            qg, kg = row(att_q_norm[j]), row(att_k_norm[j])
            proj = _mm(xp, w_qkv, gain=norm_mix[i])
            projf = _mm(xp, w_f, gain=norm_mix[i])
            qb, kf, kbf, vbf, lf, negc = _qkv_post(proj, projf, b_f, qg, kg)
            op = _flash(qb, kbf, vbf, negc[:, :H].T.reshape(H, 1, S))
            pk.append(kf.reshape(1, S, H, HEAD_DIM))
            pv.append(proj[:, 2 * D:].reshape(1, S, H, HEAD_DIM))
            plf.append(lf[:, :H].reshape(1, S, H))
            xp = _mm(op, w_out, residual=xp)
            n_s = Bd * Sd
            proj = _mm(xs, w_qkv, gain=norm_mix[i])
            projf = _mm(xs, w_f, gain=norm_mix[i])
            qb, kf, kbf, vbf, lf, _ = _qkv_post(proj, projf, b_f, qg, kg)
            lf_s = lf[:, :H].reshape(Bd, Sd, H)
            negc_s = _page_cumsum(page_table, cache_logf[j],
                                  jnp.pad(lf_s, ((0, 0), (0, PAGE_SIZE - Sd), (0, 0))))
            pad_rows = lambda a: jnp.pad(a.reshape(Bd, Sd, D), ((0, 0), (0, PAGE_SIZE - Sd), (0, 0)))
            q_rep = jnp.repeat(qb.reshape(Bd, Sd, D), H, axis=1)
            o = _decode(page_table, q_rep,
                        cache_k[j].reshape(n_pool, PAGE_SIZE, D), cache_v[j].reshape(n_pool, PAGE_SIZE, D),
                        pad_rows(kbf), pad_rows(vbf), negc_s.transpose(0, 2, 1))
            sk.append(kf.reshape(Bd, Sd, H, HEAD_DIM))
            sv.append(proj[:, 2 * D:].reshape(Bd, Sd, H, HEAD_DIM))
            slf.append(lf_s)
            xs = _mm(o.reshape(n_s, D).astype(BF16), w_out, residual=xs)
        wq, kb = _peer_weights(peer_w_q[i], peer_sub_keys[i])
        u = peer_u[i].astype(BF16)
        v = peer_v[i].astype(BF16)
        xp = _peer(xp, norm_ffn[i], wq, kb, u, v, n_heads=PH)
        xs = _peer(xs, norm_ffn[i], wq, kb, u, v, n_heads=PH)

    return (xp.reshape(1, S, D), xs.reshape(Bd, Sd, D),
            jnp.stack(pc), jnp.stack(ph), jnp.stack(pk), jnp.stack(pv), jnp.stack(plf),
            jnp.stack(sc), jnp.stack(sh), jnp.stack(sk), jnp.stack(sv), jnp.stack(slf))
```

```python
import functools
import math

import jax
import jax.numpy as jnp
from jax import lax
from jax.experimental import pallas as pl
from jax.experimental.pallas import tpu as pltpu

F32 = jnp.float32
BF16 = jnp.bfloat16

EPS = 1e-6
NEG_INF = -1e30
BIG = 3.0e38
LOG2E = math.log2(math.e)
RG_C = 8.0
CONV_W = 4
HEAD_DIM = 128
PAGE_SIZE = 128
TOPK = 16
N_KEYS = 128
LANE = 128
SUBLANE = 8
VMEM_LIMIT = 56 * 1024 * 1024


def _cparams(sem):
    return pltpu.CompilerParams(dimension_semantics=sem, vmem_limit_bytes=VMEM_LIMIT)


def _gelu(x):
    c = math.sqrt(2.0 / math.pi)
    return 0.5 * x * (1.0 + jnp.tanh(x * (c + (0.044715 * c) * (x * x))))


def _log_sigmoid(z):
    return jnp.minimum(z, 0.0) - jnp.log1p(jnp.exp(-jnp.abs(z)))


def _sigmoid(z):
    return 1.0 / (1.0 + jnp.exp(-z))


def _dot_nt(a, b):
    return lax.dot_general(a, b, (((1,), (1,)), ((), ())), preferred_element_type=F32)


def _dot_tn(a, b):
    return lax.dot_general(a, b, (((0,), (0,)), ((), ())), preferred_element_type=F32)


def _mm_kernel(*refs, norm, residual):
    it = iter(refs)
    x_ref = next(it)
    g_ref = next(it) if norm else None
    w_ref = next(it)
    res_ref = next(it) if residual else None
    out_ref = next(it)
    xn_ref = next(it) if norm else None

    if norm:
        @pl.when(pl.program_id(1) == 0)
        def _():
            xf = x_ref[...].astype(F32)
            y = xf * lax.rsqrt(jnp.mean(xf * xf, axis=-1, keepdims=True) + EPS)
            xn_ref[...] = (y * g_ref[...]).astype(BF16)
        lhs = xn_ref[...]
    else:
        lhs = x_ref[...].astype(BF16)
    acc = jnp.dot(lhs, w_ref[...], preferred_element_type=F32)
    if residual:
        acc = acc + res_ref[...]
    out_ref[...] = acc.astype(out_ref.dtype)


def _mm(x, w, *, gain=None, residual=None, out_dtype=F32, tm=512, tn=512):
    M, K = x.shape
    N = w.shape[1]
    tm = min(tm, M)
    tn = min(tn, N)
    assert M % tm == 0 and N % tn == 0
    norm = gain is not None
    in_specs = [pl.BlockSpec((tm, K), lambda i, j: (i, 0))]
    args = [x]
    if norm:
        in_specs.append(pl.BlockSpec((1, K), lambda i, j: (0, 0)))
        args.append(gain.reshape(1, K).astype(F32))
    in_specs.append(pl.BlockSpec((K, tn), lambda i, j: (0, j)))
    args.append(w)
    if residual is not None:
        in_specs.append(pl.BlockSpec((tm, tn), lambda i, j: (i, j)))
        args.append(residual)
    scratch = [pltpu.VMEM((tm, K), BF16)] if norm else []
    return pl.pallas_call(
        functools.partial(_mm_kernel, norm=norm, residual=residual is not None),
        grid=(M // tm, N // tn),
        in_specs=in_specs,
        out_specs=pl.BlockSpec((tm, tn), lambda i, j: (i, j)),
        out_shape=jax.ShapeDtypeStruct((M, N), out_dtype),
        scratch_shapes=scratch,
        compiler_params=_cparams(("parallel", "arbitrary")),
        name="mm",
    )(*args)


def _norm_kernel(x_ref, g_ref, o_ref):
    xf = x_ref[...].astype(F32)
    y = xf * lax.rsqrt(jnp.mean(xf * xf, axis=-1, keepdims=True) + EPS)
    o_ref[...] = (y * g_ref[...]).astype(o_ref.dtype)


def _rms_norm(x, gain, *, tm=512):
    M, K = x.shape
    tm = min(tm, M)
    assert M % tm == 0
    return pl.pallas_call(
        _norm_kernel,
        grid=(M // tm,),
        in_specs=[pl.BlockSpec((tm, K), lambda i: (i, 0)), pl.BlockSpec((1, K), lambda i: (0, 0))],
        out_specs=pl.BlockSpec((tm, K), lambda i: (i, 0)),
        out_shape=jax.ShapeDtypeStruct((M, K), BF16),
        compiler_params=_cparams(("parallel",)),
        name="rms_norm",
    )(x, gain.reshape(1, K).astype(F32))


def _rg_coeffs(xc, w_a, w_i, b_a, b_i, lam):
    xcb = xc.astype(BF16)
    r = _sigmoid(jnp.dot(xcb, w_a, preferred_element_type=F32) + b_a)
    i = _sigmoid(jnp.dot(xcb, w_i, preferred_element_type=F32) + b_i)
    log_a = RG_C * r * _log_sigmoid(lam)
    a = jnp.exp(log_a)
    th = jnp.tanh(log_a)
    b = jnp.sqrt(-2.0 * th / (1.0 - th)) * (i * xc)
    return a, b


def _scan_rows(a, b):
    n = a.shape[0]
    row = lax.broadcasted_iota(jnp.int32, a.shape, 0)
    s = 1
    while s < n:
        a_sh = pltpu.roll(a, s, axis=0)
        b_sh = pltpu.roll(b, s, axis=0)
        valid = row >= s
        b = jnp.where(valid, a * b_sh + b, b)
        a = jnp.where(valid, a * a_sh, a)
        s *= 2
    return a, b


def _cumsum_rows(x):
    n = x.shape[0]
    row = lax.broadcasted_iota(jnp.int32, x.shape, 0)
    s = 1
    while s < n:
        x = x + jnp.where(row >= s, pltpu.roll(x, s, axis=0), 0.0)
        s *= 2
    return x


def _rglru_prompt_kernel(gate_ref, xr_ref, cw_ref, cb_ref, wa_ref, wi_ref, ba_ref, bi_ref,
                         lam_ref, init_ref, h0_ref, y_ref, hl_ref, ext_ref, h_ref, *, tt):
    t = pl.program_id(1)

    @pl.when(t == 0)
    def _():
        ext_ref[0:SUBLANE, :] = init_ref[...]
        h_ref[...] = h0_ref[...]

    xr = xr_ref[...]
    ext_ref[SUBLANE:SUBLANE + tt, :] = xr
    xc = cb_ref[...] + xr * cw_ref[CONV_W - 1:CONV_W, :]
    for k in range(CONV_W - 1):
        xc = xc + ext_ref[pl.ds(SUBLANE - (CONV_W - 1) + k, tt), :] * cw_ref[k:k + 1, :]
    ext_ref[0:SUBLANE, :] = xr[tt - SUBLANE:tt, :]

    a, b = _rg_coeffs(xc, wa_ref[...], wi_ref[...], ba_ref[...], bi_ref[...], lam_ref[...])
    A, B = _scan_rows(a, b)
    h = A * h_ref[...] + B
    h_ref[...] = h[tt - 1:tt, :]
    hl_ref[...] = h[tt - 1:tt, :]
    y_ref[...] = (h * _gelu(gate_ref[...])).astype(y_ref.dtype)


def _rglru_prompt(proj, conv_init, h0, cw, cb, wa, wi, ba, bi, lam, *, tt=256):
    T = proj.shape[0]
    D = proj.shape[1] // 2
    nb, C = wa.shape[0], wa.shape[1]
    tt = min(tt, T)
    assert T % tt == 0 and tt % SUBLANE == 0
    vec = lambda: pl.BlockSpec((1, C), lambda n, t: (0, n))
    y, hl = pl.pallas_call(
        functools.partial(_rglru_prompt_kernel, tt=tt),
        grid=(nb, T // tt),
        in_specs=[
            pl.BlockSpec((tt, C), lambda n, t: (t, n)),
            pl.BlockSpec((tt, C), lambda n, t: (t, nb + n)),
            pl.BlockSpec((CONV_W, C), lambda n, t: (0, n)),
            vec(),
            pl.BlockSpec((None, C, C), lambda n, t: (n, 0, 0)),
            pl.BlockSpec((None, C, C), lambda n, t: (n, 0, 0)),
            vec(), vec(), vec(),
            pl.BlockSpec((SUBLANE, C), lambda n, t: (0, n)),
            vec(),
        ],
        out_specs=[pl.BlockSpec((tt, C), lambda n, t: (t, n)),
                   pl.BlockSpec((1, C), lambda n, t: (0, n))],
        out_shape=[jax.ShapeDtypeStruct((T, D), BF16), jax.ShapeDtypeStruct((1, D), F32)],
        scratch_shapes=[pltpu.VMEM((tt + SUBLANE, C), F32), pltpu.VMEM((1, C), F32)],
        compiler_params=_cparams(("parallel", "arbitrary")),
        name="rglru_prompt",
    )(proj, proj, cw, cb, wa, wi, ba, bi, lam, conv_init, h0)
    return y, hl


def _rglru_sample_kernel(gate_ref, ext_ref, cw_ref, cb_ref, wa_ref, wi_ref, ba_ref, bi_ref,
                         lam_ref, h0_ref, y_ref, hl_ref, *, steps):
    h = h0_ref[...]
    for t in range(steps):
        xc = cb_ref[...]
        for k in range(CONV_W):
            xc = xc + ext_ref[t + k] * cw_ref[k:k + 1, :]
        a, b = _rg_coeffs(xc, wa_ref[...], wi_ref[...], ba_ref[...], bi_ref[...], lam_ref[...])
        h = a * h + b
        y_ref[t] = (h * _gelu(gate_ref[t])).astype(y_ref.dtype)
    hl_ref[...] = h


def _rglru_sample(gate_tm, ext_tm, h0, cw, cb, wa, wi, ba, bi, lam):
    S, B, D = gate_tm.shape
    nb, C = wa.shape[0], wa.shape[1]
    vec = lambda: pl.BlockSpec((1, C), lambda n: (0, n))
    return pl.pallas_call(
        functools.partial(_rglru_sample_kernel, steps=S),
        grid=(nb,),
        in_specs=[
            pl.BlockSpec((S, B, C), lambda n: (0, 0, n)),
            pl.BlockSpec((S + CONV_W - 1, B, C), lambda n: (0, 0, n)),
            pl.BlockSpec((CONV_W, C), lambda n: (0, n)),
            vec(),
            pl.BlockSpec((None, C, C), lambda n: (n, 0, 0)),
            pl.BlockSpec((None, C, C), lambda n: (n, 0, 0)),
            vec(), vec(), vec(),
            pl.BlockSpec((B, C), lambda n: (0, n)),
        ],
        out_specs=[pl.BlockSpec((S, B, C), lambda n: (0, 0, n)),
                   pl.BlockSpec((B, C), lambda n: (0, n))],
        out_shape=[jax.ShapeDtypeStruct((S, B, D), BF16), jax.ShapeDtypeStruct((B, D), F32)],
        compiler_params=_cparams(("parallel",)),
        name="rglru_sample",
    )(gate_tm, ext_tm, cw, cb, wa, wi, ba, bi, lam, h0)


def _qkv_post_kernel(q_ref, k_ref, v_ref, f_ref, bf_ref, qg_ref, kg_ref, *refs, prompt):
    if prompt:
        qb_ref, kf_ref, kb_ref, vt_ref, lf_ref, bias_ref, carry_ref = refs
    else:
        qb_ref, kf_ref, kb_ref, vb_ref, lf_ref = refs
    n_heads = q_ref.shape[1] // HEAD_DIM
    q_scale = HEAD_DIM ** -0.5 * LOG2E
    for h in range(n_heads):
        sl = slice(h * HEAD_DIM, (h + 1) * HEAD_DIM)
        qh = q_ref[:, sl]
        qn = qh * lax.rsqrt(jnp.mean(qh * qh, axis=-1, keepdims=True) + EPS) * qg_ref[...]
        qb_ref[:, sl] = (qn * q_scale).astype(BF16)
        kh = k_ref[:, sl]
        kn = kh * lax.rsqrt(jnp.mean(kh * kh, axis=-1, keepdims=True) + EPS) * kg_ref[...]
        kf_ref[:, sl] = kn
        kb_ref[:, sl] = kn.astype(BF16)
        if prompt:
            vt_ref[sl, :] = v_ref[:, sl].T.astype(BF16)
    if not prompt:
        vb_ref[...] = v_ref[...].astype(BF16)
    lf = _log_sigmoid(f_ref[...] + bf_ref[...])
    lf_ref[...] = lf
    if prompt:
        @pl.when(pl.program_id(0) == 0)
        def _():
            carry_ref[...] = jnp.zeros_like(carry_ref)

        c = _cumsum_rows(lf) + carry_ref[...]
        carry_ref[...] = c[c.shape[0] - 1:, :]
        nb = c * (-LOG2E)
        for h in range(n_heads):
            bias_ref[h] = jnp.broadcast_to(nb[:, h:h + 1], (nb.shape[0], LANE))


def _qkv_post(proj, projf, b_f, q_g, k_g, *, prompt, tm=256):
    T = proj.shape[0]
    D = proj.shape[1] // 3
    H = D // HEAD_DIM
    tm = min(tm, T)
    assert T % tm == 0
    blk = lambda j: pl.BlockSpec((tm, D), lambda i: (i, j))
    small = pl.BlockSpec((tm, LANE), lambda i: (i, 0))
    vec = pl.BlockSpec((1, LANE), lambda i: (0, 0))
    out_specs = [blk(0), blk(0), blk(0)]
    out_shape = [jax.ShapeDtypeStruct((T, D), BF16), jax.ShapeDtypeStruct((T, D), F32),
                 jax.ShapeDtypeStruct((T, D), BF16)]
    if prompt:
        out_specs += [pl.BlockSpec((D, tm), lambda i: (0, i)), small,
                      pl.BlockSpec((H, tm, LANE), lambda i: (0, i, 0))]
        out_shape += [jax.ShapeDtypeStruct((D, T), BF16), jax.ShapeDtypeStruct((T, LANE), F32),
                      jax.ShapeDtypeStruct((H, T, LANE), F32)]
        scratch = [pltpu.VMEM((1, LANE), F32)]
    else:
        out_specs += [blk(0), small]
        out_shape += [jax.ShapeDtypeStruct((T, D), BF16), jax.ShapeDtypeStruct((T, LANE), F32)]
        scratch = []
    return pl.pallas_call(
        functools.partial(_qkv_post_kernel, prompt=prompt),
        grid=(T // tm,),
        in_specs=[blk(0), blk(1), blk(2), small, vec, vec, vec],
        out_specs=out_specs,
        out_shape=out_shape,
        scratch_shapes=scratch,
        compiler_params=_cparams(("arbitrary",)),
        name="qkv_post",
    )(proj, proj, proj, projf, b_f, q_g, k_g)


def _flash_kernel(q_ref, k_ref, vt_ref, b_ref, o_ref, s_ref, m_ref, l_ref, acc_ref, *, tq):
    qi = pl.program_id(1)
    q = q_ref[...]
    m_ref[...] = jnp.full_like(m_ref, NEG_INF)
    l_ref[...] = jnp.zeros_like(l_ref)
    acc_ref[...] = jnp.zeros_like(acc_ref)

    def scores(i):
        k0 = pl.multiple_of(i * tq, tq)
        bias = b_ref[pl.ds(k0, tq), :]
        return _dot_nt(k_ref[pl.ds(k0, tq), :], q) + jnp.concatenate([bias] * (tq // LANE), axis=1)

    def update(s, i):
        k0 = pl.multiple_of(i * tq, tq)
        m_prev = m_ref[...]
        m_new = jnp.maximum(m_prev, jnp.max(s, axis=0, keepdims=True))
        alpha = jnp.exp2(m_prev - m_new)
        p = jnp.exp2(s - m_new)
        l_ref[...] = alpha * l_ref[...] + jnp.sum(p, axis=0, keepdims=True)
        acc_ref[...] = alpha * acc_ref[...] + jnp.dot(vt_ref[:, pl.ds(k0, tq)], p.astype(BF16),
                                                     preferred_element_type=F32)
        m_ref[...] = m_new

    s_ref[0] = scores(0)

    def body(i, carry):
        s_next = scores(i + 1)
        update(s_ref[i % 2], i)
        s_ref[(i + 1) % 2] = s_next
        return carry

    lax.fori_loop(0, qi, body, 0)
    s = s_ref[qi % 2]
    key = lax.broadcasted_iota(jnp.int32, s.shape, 0)
    qry = lax.broadcasted_iota(jnp.int32, s.shape, 1)
    update(jnp.where(key <= qry, s, NEG_INF), qi)
    o_ref[...] = (acc_ref[...] / l_ref[...]).T.astype(o_ref.dtype)


def _flash(q, k, vt, bias, *, tq=512):
    T, D = q.shape
    H = D // HEAD_DIM
    tq = min(tq, T)
    assert T % tq == 0
    return pl.pallas_call(
        functools.partial(_flash_kernel, tq=tq),
        grid=(H, T // tq),
        in_specs=[
            pl.BlockSpec((tq, HEAD_DIM), lambda h, i: (i, h)),
            pl.BlockSpec((T, HEAD_DIM), lambda h, i: (0, h)),
            pl.BlockSpec((HEAD_DIM, T), lambda h, i: (h, 0)),
            pl.BlockSpec((None, T, LANE), lambda h, i: (h, 0, 0)),
        ],
        out_specs=pl.BlockSpec((tq, HEAD_DIM), lambda h, i: (i, h)),
        out_shape=jax.ShapeDtypeStruct((T, D), BF16),
        scratch_shapes=[pltpu.VMEM((2, tq, tq), F32), pltpu.VMEM((1, tq), F32), pltpu.VMEM((1, tq), F32),
                        pltpu.VMEM((HEAD_DIM, tq), F32)],
        compiler_params=_cparams(("parallel", "arbitrary")),
        name="flash",
    )(q, k, vt, bias)


def _decode_kernel(pt_ref, q_ref, *refs, pps, n_steps, n_heads):
    kc = refs[0:pps]
    vc = refs[pps:2 * pps]
    lc = refs[2 * pps:3 * pps]
    kn_ref, vn_ref, ln_ref, o_ref, m_ref, l_ref, acc_ref, carry_ref, hm_ref = refs[3 * pps:]
    p = pl.program_id(1)
    width = hm_ref.shape[1]

    @pl.when(p == 0)
    def _():
        m_ref[...] = jnp.full_like(m_ref, NEG_INF)
        l_ref[...] = jnp.zeros_like(l_ref)
        acc_ref[...] = jnp.zeros_like(acc_ref)
        carry_ref[...] = jnp.zeros_like(carry_ref)
        row = lax.broadcasted_iota(jnp.int32, hm_ref.shape, 0)
        col = lax.broadcasted_iota(jnp.int32, hm_ref.shape, 1)
        hm_ref[...] = jnp.where((col % n_heads) == (row % n_heads), 0.0, NEG_INF)

    def page_bias(lf_ref):
        x = jnp.broadcast_to(lf_ref[...], carry_ref.shape)
        lane = lax.broadcasted_iota(jnp.int32, x.shape, 1)
        tot = x
        sh = n_heads
        while sh < width:
            x = x + jnp.where(lane >= sh, pltpu.roll(x, sh, axis=1), 0.0)
            tot = tot + pltpu.roll(tot, sh, axis=1)
            sh *= 2
        c = x + carry_ref[...]
        carry_ref[...] = carry_ref[...] + tot
        return c[0:1, :] * (-LOG2E)

    def attend(ks, vs, biases, causal):
        q = q_ref[...]
        ss = []
        for kb, b in zip(ks, biases):
            s = _dot_nt(q, kb) + (hm_ref[...] + b)
            if causal:
                row = lax.broadcasted_iota(jnp.int32, s.shape, 0)
                col = lax.broadcasted_iota(jnp.int32, s.shape, 1)
                s = jnp.where(col // n_heads <= row // n_heads, s, NEG_INF)
            ss.append(s)
        m_prev = m_ref[...]
        m_new = m_prev
        for s in ss:
            m_new = jnp.maximum(m_new, jnp.max(s, axis=-1, keepdims=True))
        alpha = jnp.exp2(m_prev - m_new)
        l = alpha * l_ref[...]
        acc = alpha * acc_ref[...]
        for s, vb in zip(ss, vs):
            pr = jnp.exp2(s - m_new)
            l = l + jnp.sum(pr, axis=-1, keepdims=True)
            acc = acc + jnp.dot(pr.astype(BF16), vb, preferred_element_type=F32)
        m_ref[...] = m_new
        l_ref[...] = l
        acc_ref[...] = acc

    @pl.when(p < n_steps)
    def _():
        biases = [page_bias(r) for r in lc]
        attend([r[...].astype(BF16) for r in kc], [r[...].astype(BF16) for r in vc], biases, False)

    @pl.when(p == n_steps)
    def _():
        attend([kn_ref[...]], [vn_ref[...]], [page_bias(ln_ref)], True)
        o_ref[...] = acc_ref[...] / l_ref[...]


def _decode(page_ids, q2, cache_k, cache_v, cache_lf, k_new, v_new, lf_new, *, pps=4):
    Bd, P = page_ids.shape
    R = q2.shape[1]
    W = cache_k.shape[1]
    H = W // PAGE_SIZE
    assert P % pps == 0
    n_steps = P // pps
    last = n_steps - 1

    def page_map(i):
        return lambda b, p, pt: (pt[b, jnp.minimum(p, last) * pps + i], 0, 0)

    seq_map = lambda b, p, pt: (b, 0, 0)
    in_specs = [pl.BlockSpec((None, R, HEAD_DIM), seq_map)]
    in_specs += [pl.BlockSpec((None, W, HEAD_DIM), page_map(i)) for i in range(pps)]
    in_specs += [pl.BlockSpec((None, W, HEAD_DIM), page_map(i)) for i in range(pps)]
    in_specs += [pl.BlockSpec((None, 1, W), page_map(i)) for i in range(pps)]
    in_specs += [pl.BlockSpec((None, W, HEAD_DIM), seq_map), pl.BlockSpec((None, W, HEAD_DIM), seq_map),
                 pl.BlockSpec((None, 1, W), seq_map)]
    grid_spec = pltpu.PrefetchScalarGridSpec(
        num_scalar_prefetch=1,
        grid=(Bd, n_steps + 1),
        in_specs=in_specs,
        out_specs=pl.BlockSpec((None, R, HEAD_DIM), seq_map),
        scratch_shapes=[pltpu.VMEM((R, 1), F32), pltpu.VMEM((R, 1), F32), pltpu.VMEM((R, HEAD_DIM), F32),
                        pltpu.VMEM((SUBLANE, W), F32), pltpu.VMEM((R, W), F32)],
    )
    return pl.pallas_call(
        functools.partial(_decode_kernel, pps=pps, n_steps=n_steps, n_heads=H),
        grid_spec=grid_spec,
        out_shape=jax.ShapeDtypeStruct((Bd, R, HEAD_DIM), F32),
        compiler_params=_cparams(("parallel", "arbitrary")),
        name="decode",
    )(page_ids, q2, *([cache_k] * pps), *([cache_v] * pps), *([cache_lf] * pps), k_new, v_new, lf_new)


def _cmpx(v, i, j):
    hi = jnp.maximum(v[i], v[j])
    lo = jnp.minimum(v[i], v[j])
    v[i], v[j] = hi, lo


def _bitonic_merge_desc(v):
    n = len(v)
    j = n // 2
    while j >= 1:
        for i in range(n):
            if i & j == 0:
                _cmpx(v, i, i | j)
        j //= 2
    return v


def _sort_desc(v):
    v = list(v)
    n = len(v)
    k = 2
    while k <= n:
        j = k // 2
        while j >= 1:
            for i in range(n):
                l = i ^ j
                if l > i:
                    if (i & k) == 0:
                        _cmpx(v, i, l)
                    else:
                        _cmpx(v, l, i)
            j //= 2
        k *= 2
    return v


def _top16_sorted(v):
    groups = [_sort_desc(v[g:g + TOPK]) for g in range(0, len(v), TOPK)]
    while len(groups) > 1:
        nxt = []
        for g in range(0, len(groups), 2):
            a, b = groups[g], groups[g + 1]
            c = [jnp.maximum(a[i], b[TOPK - 1 - i]) for i in range(TOPK)]
            nxt.append(_bitonic_merge_desc(c))
        groups = nxt
    return groups[0]


def _peer_route_kernel(q_ref, kb_ref, thr_ref, e1_ref, s2_ref, e2_ref, *, n_heads):
    nrow = n_heads * N_KEYS
    tb = q_ref.shape[0]
    tops = []
    per_head = []
    for p in range(2):
        qp = q_ref[:, p * nrow:(p + 1) * nrow]
        res = _dot_nt(kb_ref[p], qp)
        srt = res[0:nrow].reshape(N_KEYS, n_heads, tb)
        per_head.append(res[nrow:2 * nrow].reshape(n_heads, N_KEYS, tb))
        tops.append(_top16_sorted([srt[k] for k in range(N_KEYS)]))
    v1, v2 = tops
    pairs = [(a, b) for a in range(TOPK) for b in range(TOPK) if (a + 1) * (b + 1) <= TOPK]
    cand = {ab: v1[ab[0]] + v2[ab[1]] for ab in pairs}
    pad = jnp.full_like(v1[0], -BIG)
    top = _top16_sorted([cand[ab] for ab in pairs] + [pad] * (4 * TOPK - len(pairs)))
    tau = top[TOPK - 1]
    z = jnp.ones_like(tau)
    for k in range(1, TOPK):
        z = z + jnp.exp(top[k] - top[0])
    inv_z = 1.0 / z
    thr_rank = []
    for a in range(TOPK):
        t = jnp.full_like(tau, BIG)
        for b in range(TOPK):
            if (a, b) in cand:
                t = jnp.minimum(t, jnp.where(cand[(a, b)] >= tau, v2[b], BIG))
        thr_rank.append(t)
    s1, s2 = per_head
    s2_ref[...] = s2
    for h in range(n_heads):
        row = lambda x: x[h:h + 1, :]
        s1h = s1[h]
        thr = jnp.full_like(s1h, BIG)
        for a in range(TOPK):
            thr = jnp.minimum(thr, jnp.where(s1h >= row(v1[a]), row(thr_rank[a]), BIG))
        thr_ref[h] = thr
        e1_ref[h] = jnp.exp(s1h - row(v1[0]))
        e2_ref[h] = jnp.exp(s2[h] - row(v2[0])) * row(inv_z)


def _peer_route(q, kb, *, n_heads, tb=128):
    N = q.shape[0]
    tb = min(tb, N)
    assert N % tb == 0
    big = lambda: pl.BlockSpec((n_heads, N_KEYS, tb), lambda i: (0, 0, i))
    shp = jax.ShapeDtypeStruct((n_heads, N_KEYS, N), F32)
    return pl.pallas_call(
        functools.partial(_peer_route_kernel, n_heads=n_heads),
        grid=(N // tb,),
        in_specs=[pl.BlockSpec((tb, q.shape[1]), lambda i: (i, 0)),
                  pl.BlockSpec(kb.shape, lambda i: (0, 0, 0))],
        out_specs=[big(), big(), big(), big()],
        out_shape=[shp, shp, shp, shp],
        compiler_params=_cparams(("parallel",)),
        name="peer_route",
    )(q, kb)


def _peer_dense_kernel(xn_ref, u_ref, v_ref, thr_ref, e1_ref, s2_ref, e2_ref, res_ref,
                       out_ref, r_ref, *, n_heads, rows_per_step):
    e = pl.program_id(1)
    slab = 4 * SUBLANE

    @pl.when(e == 0)
    def _():
        out_ref[...] = res_ref[...]

    def routing_row(r, carry):
        i1 = e * rows_per_step + r
        thr_rows = [thr_ref[h, pl.ds(i1, 1), :] for h in range(n_heads)]
        e1_rows = [e1_ref[h, pl.ds(i1, 1), :] for h in range(n_heads)]
        for b in range(N_KEYS // slab):
            rs = slice(b * slab, (b + 1) * slab)
            acc = None
            for h in range(n_heads):
                sel = jnp.where(s2_ref[h, rs, :] >= thr_rows[h], e2_ref[h, rs, :], 0.0)
                term = sel * e1_rows[h]
                acc = term if acc is None else acc + term
            r_ref[pl.ds(pl.multiple_of(r * N_KEYS + b * slab, slab), slab), :] = acc
        return carry

    lax.fori_loop(0, rows_per_step, routing_row, 0)
    act = _dot_nt(u_ref[...], xn_ref[...])
    wt = (_gelu(act) * r_ref[...]).astype(BF16)
    out_ref[...] += _dot_tn(wt, v_ref[...])


def _peer_dense(xn, u, v, thr, e1, s2, e2, resid, *, tb=512, te=1024):
    N, D = xn.shape
    E = u.shape[0]
    n_heads = thr.shape[0]
    tb = min(tb, N)
    assert N % tb == 0 and E % te == 0 and te % (2 * N_KEYS) == 0
    rps = te // N_KEYS
    once = pl.Buffered(1)
    tok = lambda: pl.BlockSpec((n_heads, N_KEYS, tb), lambda i, e: (0, 0, i), pipeline_mode=once)
    return pl.pallas_call(
        functools.partial(_peer_dense_kernel, n_heads=n_heads, rows_per_step=rps),
        grid=(N // tb, E // te),
        in_specs=[
            pl.BlockSpec((tb, D), lambda i, e: (i, 0), pipeline_mode=once),
            pl.BlockSpec((te, D), lambda i, e: (e, 0)),
            pl.BlockSpec((te, D), lambda i, e: (e, 0)),
            tok(), tok(), tok(), tok(),
            pl.BlockSpec((tb, D), lambda i, e: (i, 0), pipeline_mode=once),
        ],
        out_specs=pl.BlockSpec((tb, D), lambda i, e: (i, 0)),
        out_shape=jax.ShapeDtypeStruct((N, D), F32),
        scratch_shapes=[pltpu.VMEM((te, tb), F32)],
        compiler_params=_cparams(("parallel", "arbitrary")),
        name="peer_dense",
    )(xn, u, v, thr, e1, s2, e2, resid)


def _peer_weights(w_q, sub_keys):
    D = w_q.shape[0]
    H = sub_keys.shape[0]
    half = sub_keys.shape[-1]
    wq = w_q.reshape(D, H, 2, half).transpose(0, 2, 1, 3).reshape(D, 2 * H * half).astype(BF16)
    eye = jnp.eye(H, dtype=sub_keys.dtype)
    sk = sub_keys.transpose(1, 0, 2, 3)
    bd = sk[:, :, :, None, :] * eye[None, :, None, :, None]
    hk = bd.reshape(2, H * N_KEYS, H * half)
    kh = bd.transpose(0, 2, 1, 3, 4).reshape(2, N_KEYS * H, H * half)
    return wq, jnp.concatenate([kh, hk], axis=1).astype(BF16)


def _peer(x, gain, wq, kb, u, v, *, n_heads):
    xn = _rms_norm(x, gain)
    q = _mm(xn, wq, out_dtype=BF16)
    thr, e1, s2, e2 = _peer_route(q, kb, n_heads=n_heads)
    return _peer_dense(xn, u, v, thr, e1, s2, e2, x)


def _pad_lanes(a, width=LANE):
    return jnp.pad(a, [(0, 0)] * (a.ndim - 1) + [(0, width - a.shape[-1])])


def kernel(x_prompt, x_sample, state_conv, state_h, cache_k, cache_v, cache_logf, page_table,
           norm_mix, norm_ffn,
           rec_w_in, rec_conv_w, rec_conv_b, rec_w_a, rec_b_a, rec_w_i, rec_b_i, rec_lambda, rec_w_out,
           att_w_in, att_b_f, att_q_norm, att_k_norm, att_w_out,
           peer_w_q, peer_sub_keys, peer_u, peer_v):
    Bp, S, D = x_prompt.shape
    Bd, Sd, _ = x_sample.shape
    assert Bp == 1
    H = D // HEAD_DIM
    PH = peer_sub_keys.shape[1]
    depth = norm_mix.shape[0]
    n_att, n_pool = cache_k.shape[0], cache_k.shape[1]
    page_w = PAGE_SIZE * H

    xp = x_prompt.reshape(S, D)
    xs = x_sample.reshape(Bd * Sd, D)
    row = lambda a: a.reshape(1, -1).astype(F32)
    cache_k2 = cache_k.reshape(n_att * n_pool, page_w, HEAD_DIM)
    cache_v2 = cache_v.reshape(n_att * n_pool, page_w, HEAD_DIM)
    cache_lf2 = cache_logf.reshape(n_att * n_pool, 1, page_w)

    pc, ph, pk, pv, plf = [], [], [], [], []
    sc, sh, sk, sv, slf = [], [], [], [], []
    for i in range(depth):
        j = i // 2
        if i % 2 == 0:
            w_in = rec_w_in[j].astype(BF16)
            w_out = rec_w_out[j].astype(BF16)
            wa = rec_w_a[j].astype(BF16)
            wi = rec_w_i[j].astype(BF16)
            cw, cb = rec_conv_w[j], row(rec_conv_b[j])
            ba, bi, lam = row(rec_b_a[j]), row(rec_b_i[j]), row(rec_lambda[j])
            proj_p = _mm(xp, w_in, gain=norm_mix[i])
            yp, hl_p = _rglru_prompt(proj_p, jnp.zeros((SUBLANE, D), F32), jnp.zeros((1, D), F32),
                                     cw, cb, wa, wi, ba, bi, lam)
            pc.append(proj_p[S - (CONV_W - 1):, D:].reshape(1, CONV_W - 1, D))
            ph.append(hl_p)
            xp = _mm(yp, w_out, residual=xp)
            proj_s = _mm(xs, w_in, gain=norm_mix[i]).reshape(Bd, Sd, 2 * D)
            ext = jnp.concatenate([state_conv[j], proj_s[..., D:]], axis=1)
            ys_tm, hl_s = _rglru_sample(proj_s[..., :D].transpose(1, 0, 2), ext.transpose(1, 0, 2),
                                        state_h[j], cw, cb, wa, wi, ba, bi, lam)
            sc.append(ext[:, Sd:])
            sh.append(hl_s)
            xs = _mm(ys_tm.transpose(1, 0, 2).reshape(Bd * Sd, D), w_out, residual=xs)
        else:
            w_qkv = att_w_in[j][:, :3 * D].astype(BF16)
            w_f = _pad_lanes(att_w_in[j][:, 3 * D:]).astype(BF16)
            w_out = att_w_out[j].astype(BF16)
            b_f = _pad_lanes(row(att_b_f[j]))
            qg, kg = row(att_q_norm[j]), row(att_k_norm[j])
            proj = _mm(xp, w_qkv, gain=norm_mix[i])
            projf = _mm(xp, w_f, gain=norm_mix[i])
            qb, kf, kbf, vt, lf, bias = _qkv_post(proj, projf, b_f, qg, kg, prompt=True)
            op = _flash(qb, kbf, vt, bias)
            pk.append(kf.reshape(1, S, H, HEAD_DIM))
            pv.append(proj[:, 2 * D:].reshape(1, S, H, HEAD_DIM))
            plf.append(lf[:, :H].reshape(1, S, H))
            xp = _mm(op, w_out, residual=xp)
            n_s = Bd * Sd
            proj = _mm(xs, w_qkv, gain=norm_mix[i])
            projf = _mm(xs, w_f, gain=norm_mix[i])
            qb, kf, kbf, vbf, lf = _qkv_post(proj, projf, b_f, qg, kg, prompt=False)
            lf_s = lf[:, :H].reshape(Bd, Sd, H)
            as_page = lambda a: jnp.pad(a.reshape(Bd, Sd * H, -1),
                                        ((0, 0), (0, page_w - Sd * H), (0, 0)))
            o = _decode(page_table + j * n_pool, qb.reshape(Bd, Sd * H, HEAD_DIM),
                        cache_k2, cache_v2, cache_lf2,
                        as_page(kbf), as_page(vbf), as_page(lf_s).reshape(Bd, 1, page_w))
            sk.append(kf.reshape(Bd, Sd, H, HEAD_DIM))
            sv.append(proj[:, 2 * D:].reshape(Bd, Sd, H, HEAD_DIM))
            slf.append(lf_s)
            xs = _mm(o.reshape(n_s, D), w_out, residual=xs)
        wq, kb = _peer_weights(peer_w_q[i], peer_sub_keys[i])
        u = peer_u[i].astype(BF16)
        v = peer_v[i].astype(BF16)
        xp = _peer(xp, norm_ffn[i], wq, kb, u, v, n_heads=PH)
        xs = _peer(xs, norm_ffn[i], wq, kb, u, v, n_heads=PH)

    return (xp.reshape(1, S, D), xs.reshape(Bd, Sd, D),
            jnp.stack(pc), jnp.stack(ph), jnp.stack(pk), jnp.stack(pv), jnp.stack(plf),
            jnp.stack(sc), jnp.stack(sh), jnp.stack(sk), jnp.stack(sv), jnp.stack(slf))
```

```python
import functools
import math

import jax
import jax.numpy as jnp
from jax import lax
from jax.experimental import pallas as pl
from jax.experimental.pallas import tpu as pltpu

F32 = jnp.float32
BF16 = jnp.bfloat16

EPS = 1e-6
NEG_INF = -1e30
BIG = 3.0e38
LOG2E = math.log2(math.e)
RG_C = 8.0
CONV_W = 4
HEAD_DIM = 128
PAGE_SIZE = 128
TOPK = 16
N_KEYS = 128
LANE = 128
SUBLANE = 8
VMEM_LIMIT = 56 * 1024 * 1024


def _cparams(sem, flags=None):
    return pltpu.CompilerParams(dimension_semantics=sem, vmem_limit_bytes=VMEM_LIMIT, flags=flags)


def _largest_tile(n, target, unit):
    t = min(target, n) // unit * unit
    while n % t:
        t -= unit
    return t


def _gelu(x):
    c = math.sqrt(2.0 / math.pi)
    return 0.5 * x * (1.0 + jnp.tanh(x * (c + (0.044715 * c) * (x * x))))


def _log_sigmoid(z):
    return jnp.minimum(z, 0.0) - jnp.log1p(jnp.exp(-jnp.abs(z)))


def _sigmoid(z):
    return 1.0 / (1.0 + jnp.exp(-z))


def _dot_nt(a, b):
    return lax.dot_general(a, b, (((1,), (1,)), ((), ())), preferred_element_type=F32)


def _dot_tn(a, b):
    return lax.dot_general(a, b, (((0,), (0,)), ((), ())), preferred_element_type=F32)


def _mm_kernel(*refs, norm, residual):
    it = iter(refs)
    x_ref = next(it)
    g_ref = next(it) if norm else None
    w_ref = next(it)
    res_ref = next(it) if residual else None
    out_ref = next(it)
    xn_ref = next(it) if norm else None

    if norm:
        @pl.when(pl.program_id(1) == 0)
        def _():
            xf = x_ref[...].astype(F32)
            y = xf * lax.rsqrt(jnp.mean(xf * xf, axis=-1, keepdims=True) + EPS)
            xn_ref[...] = (y * g_ref[...]).astype(BF16)
        lhs = xn_ref[...]
    else:
        lhs = x_ref[...].astype(BF16)
    acc = jnp.dot(lhs, w_ref[...], preferred_element_type=F32)
    if residual:
        acc = acc + res_ref[...]
    out_ref[...] = acc.astype(out_ref.dtype)


def _mm(x, w, *, gain=None, residual=None, out_dtype=F32, tm=1024, tn=1024):
    M, K = x.shape
    N = w.shape[1]
    tm = _largest_tile(M, tm, SUBLANE)
    tn = _largest_tile(N, tn, LANE)
    norm = gain is not None
    in_specs = [pl.BlockSpec((tm, K), lambda i, j: (i, 0))]
    args = [x]
    if norm:
        in_specs.append(pl.BlockSpec((1, K), lambda i, j: (0, 0)))
        args.append(gain.reshape(1, K).astype(F32))
    in_specs.append(pl.BlockSpec((K, tn), lambda i, j: (0, j)))
    args.append(w)
    if residual is not None:
        in_specs.append(pl.BlockSpec((tm, tn), lambda i, j: (i, j)))
        args.append(residual)
    scratch = [pltpu.VMEM((tm, K), BF16)] if norm else []
    return pl.pallas_call(
        functools.partial(_mm_kernel, norm=norm, residual=residual is not None),
        grid=(M // tm, N // tn),
        in_specs=in_specs,
        out_specs=pl.BlockSpec((tm, tn), lambda i, j: (i, j)),
        out_shape=jax.ShapeDtypeStruct((M, N), out_dtype),
        scratch_shapes=scratch,
        compiler_params=_cparams(("parallel", "arbitrary")),
        name="mm",
    )(*args)


def _cast_kernel(x_ref, o_ref):
    o_ref[...] = x_ref[...].astype(o_ref.dtype)


def _to_bf16(w, layer, *, rows=1024):
    _, M, K = w.shape
    rows = min(rows, M)
    assert M % rows == 0
    return pl.pallas_call(
        _cast_kernel,
        grid=(M // rows,),
        in_specs=[pl.BlockSpec((None, rows, K), lambda i: (layer, i, 0))],
        out_specs=pl.BlockSpec((rows, K), lambda i: (i, 0)),
        out_shape=jax.ShapeDtypeStruct((M, K), BF16),
        compiler_params=_cparams(("parallel",)),
        name="to_bf16",
    )(w)


def _norm_kernel(x_ref, g_ref, o_ref):
    xf = x_ref[...].astype(F32)
    y = xf * lax.rsqrt(jnp.mean(xf * xf, axis=-1, keepdims=True) + EPS)
    o_ref[...] = (y * g_ref[...]).astype(o_ref.dtype)


def _rms_norm(x, gain, *, tm=512):
    M, K = x.shape
    tm = min(tm, M)
    assert M % tm == 0
    return pl.pallas_call(
        _norm_kernel,
        grid=(M // tm,),
        in_specs=[pl.BlockSpec((tm, K), lambda i: (i, 0)), pl.BlockSpec((1, K), lambda i: (0, 0))],
        out_specs=pl.BlockSpec((tm, K), lambda i: (i, 0)),
        out_shape=jax.ShapeDtypeStruct((M, K), BF16),
        compiler_params=_cparams(("parallel",)),
        name="rms_norm",
    )(x, gain.reshape(1, K).astype(F32))


def _rg_coeffs(xc, w_a, w_i, b_a, b_i, lam):
    xcb = xc.astype(BF16)
    r = _sigmoid(jnp.dot(xcb, w_a, preferred_element_type=F32) + b_a)
    i = _sigmoid(jnp.dot(xcb, w_i, preferred_element_type=F32) + b_i)
    log_a = RG_C * r * _log_sigmoid(lam)
    a = jnp.exp(log_a)
    th = jnp.tanh(log_a)
    b = jnp.sqrt(-2.0 * th / (1.0 - th)) * (i * xc)
    return a, b


def _scan_rows(a, b):
    n = a.shape[0]
    row = lax.broadcasted_iota(jnp.int32, a.shape, 0)
    s = 1
    while s < n:
        a_sh = pltpu.roll(a, s, axis=0)
        b_sh = pltpu.roll(b, s, axis=0)
        valid = row >= s
        b = jnp.where(valid, a * b_sh + b, b)
        a = jnp.where(valid, a * a_sh, a)
        s *= 2
    return a, b


def _cumsum_rows(x):
    n = x.shape[0]
    row = lax.broadcasted_iota(jnp.int32, x.shape, 0)
    s = 1
    while s < n:
        x = x + jnp.where(row >= s, pltpu.roll(x, s, axis=0), 0.0)
        s *= 2
    return x


def _rglru_prompt_kernel(gate_ref, xr_ref, cw_ref, cb_ref, wa_ref, wi_ref, ba_ref, bi_ref,
                         lam_ref, init_ref, h0_ref, y_ref, hl_ref, ext_ref, h_ref, *, tt):
    t = pl.program_id(1)

    @pl.when(t == 0)
    def _():
        ext_ref[0:SUBLANE, :] = init_ref[...]
        h_ref[...] = h0_ref[...]

    xr = xr_ref[...]
    ext_ref[SUBLANE:SUBLANE + tt, :] = xr
    xc = cb_ref[...] + xr * cw_ref[CONV_W - 1:CONV_W, :]
    for k in range(CONV_W - 1):
        xc = xc + ext_ref[pl.ds(SUBLANE - (CONV_W - 1) + k, tt), :] * cw_ref[k:k + 1, :]
    ext_ref[0:SUBLANE, :] = xr[tt - SUBLANE:tt, :]

    a, b = _rg_coeffs(xc, wa_ref[...], wi_ref[...], ba_ref[...], bi_ref[...], lam_ref[...])
    A, B = _scan_rows(a, b)
    h = A * h_ref[...] + B
    h_ref[...] = h[tt - 1:tt, :]
    hl_ref[...] = h[tt - 1:tt, :]
    y_ref[...] = (h * _gelu(gate_ref[...])).astype(y_ref.dtype)


def _rglru_prompt(proj, conv_init, h0, cw, cb, wa, wi, ba, bi, lam, *, tt=256):
    T = proj.shape[0]
    D = proj.shape[1] // 2
    nb, C = wa.shape[0], wa.shape[1]
    tt = min(tt, T)
    assert T % tt == 0 and tt % SUBLANE == 0
    vec = lambda: pl.BlockSpec((1, C), lambda n, t: (0, n))
    y, hl = pl.pallas_call(
        functools.partial(_rglru_prompt_kernel, tt=tt),
        grid=(nb, T // tt),
        in_specs=[
            pl.BlockSpec((tt, C), lambda n, t: (t, n)),
            pl.BlockSpec((tt, C), lambda n, t: (t, nb + n)),
            pl.BlockSpec((CONV_W, C), lambda n, t: (0, n)),
            vec(),
            pl.BlockSpec((None, C, C), lambda n, t: (n, 0, 0)),
            pl.BlockSpec((None, C, C), lambda n, t: (n, 0, 0)),
            vec(), vec(), vec(),
            pl.BlockSpec((SUBLANE, C), lambda n, t: (0, n)),
            vec(),
        ],
        out_specs=[pl.BlockSpec((tt, C), lambda n, t: (t, n)),
                   pl.BlockSpec((1, C), lambda n, t: (0, n))],
        out_shape=[jax.ShapeDtypeStruct((T, D), BF16), jax.ShapeDtypeStruct((1, D), F32)],
        scratch_shapes=[pltpu.VMEM((tt + SUBLANE, C), F32), pltpu.VMEM((1, C), F32)],
        compiler_params=_cparams(("parallel", "arbitrary")),
        name="rglru_prompt",
    )(proj, proj, cw, cb, wa, wi, ba, bi, lam, conv_init, h0)
    return y, hl


def _rglru_sample_kernel(gate_ref, ext_ref, cw_ref, cb_ref, wa_ref, wi_ref, ba_ref, bi_ref,
                         lam_ref, h0_ref, y_ref, hl_ref, *, steps):
    h = h0_ref[...]
    for t in range(steps):
        xc = cb_ref[...]
        for k in range(CONV_W):
            xc = xc + ext_ref[t + k] * cw_ref[k:k + 1, :]
        a, b = _rg_coeffs(xc, wa_ref[...], wi_ref[...], ba_ref[...], bi_ref[...], lam_ref[...])
        h = a * h + b
        y_ref[t] = (h * _gelu(gate_ref[t])).astype(y_ref.dtype)
    hl_ref[...] = h


def _rglru_sample(gate_tm, ext_tm, h0, cw, cb, wa, wi, ba, bi, lam):
    S, B, D = gate_tm.shape
    nb, C = wa.shape[0], wa.shape[1]
    vec = lambda: pl.BlockSpec((1, C), lambda n: (0, n))
    return pl.pallas_call(
        functools.partial(_rglru_sample_kernel, steps=S),
        grid=(nb,),
        in_specs=[
            pl.BlockSpec((S, B, C), lambda n: (0, 0, n)),
            pl.BlockSpec((S + CONV_W - 1, B, C), lambda n: (0, 0, n)),
            pl.BlockSpec((CONV_W, C), lambda n: (0, n)),
            vec(),
            pl.BlockSpec((None, C, C), lambda n: (n, 0, 0)),
            pl.BlockSpec((None, C, C), lambda n: (n, 0, 0)),
            vec(), vec(), vec(),
            pl.BlockSpec((B, C), lambda n: (0, n)),
        ],
        out_specs=[pl.BlockSpec((S, B, C), lambda n: (0, 0, n)),
                   pl.BlockSpec((B, C), lambda n: (0, n))],
        out_shape=[jax.ShapeDtypeStruct((S, B, D), BF16), jax.ShapeDtypeStruct((B, D), F32)],
        compiler_params=_cparams(("parallel",)),
        name="rglru_sample",
    )(gate_tm, ext_tm, cw, cb, wa, wi, ba, bi, lam, h0)


def _qkv_post_kernel(q_ref, k_ref, v_ref, f_ref, bf_ref, qg_ref, kg_ref, *refs, prompt):
    if prompt:
        qb_ref, kf_ref, kb_ref, vt_ref, lf_ref, bias_ref, carry_ref = refs
    else:
        qb_ref, kf_ref, kb_ref, vb_ref, lf_ref = refs
    n_heads = q_ref.shape[1] // HEAD_DIM
    q_scale = HEAD_DIM ** -0.5 * LOG2E
    for h in range(n_heads):
        sl = slice(h * HEAD_DIM, (h + 1) * HEAD_DIM)
        qh = q_ref[:, sl]
        qn = qh * lax.rsqrt(jnp.mean(qh * qh, axis=-1, keepdims=True) + EPS) * qg_ref[...]
        qb_ref[:, sl] = (qn * q_scale).astype(BF16)
        kh = k_ref[:, sl]
        kn = kh * lax.rsqrt(jnp.mean(kh * kh, axis=-1, keepdims=True) + EPS) * kg_ref[...]
        kf_ref[:, sl] = kn
        kb_ref[:, sl] = kn.astype(BF16)
        if prompt:
            vt_ref[sl, :] = v_ref[:, sl].T.astype(BF16)
    if not prompt:
        vb_ref[...] = v_ref[...].astype(BF16)
    lf = _log_sigmoid(f_ref[...] + bf_ref[...])
    lf_ref[...] = lf
    if prompt:
        @pl.when(pl.program_id(0) == 0)
        def _():
            carry_ref[...] = jnp.zeros_like(carry_ref)

        c = _cumsum_rows(lf) + carry_ref[...]
        carry_ref[...] = c[c.shape[0] - 1:, :]
        nb = c * (-LOG2E)
        for h in range(n_heads):
            bias_ref[h] = jnp.broadcast_to(nb[:, h:h + 1], (nb.shape[0], LANE))


def _qkv_post(proj, projf, b_f, q_g, k_g, *, prompt, tm=256):
    T = proj.shape[0]
    D = proj.shape[1] // 3
    H = D // HEAD_DIM
    tm = min(tm, T)
    assert T % tm == 0
    blk = lambda j: pl.BlockSpec((tm, D), lambda i: (i, j))
    small = pl.BlockSpec((tm, LANE), lambda i: (i, 0))
    vec = pl.BlockSpec((1, LANE), lambda i: (0, 0))
    out_specs = [blk(0), blk(0), blk(0)]
    out_shape = [jax.ShapeDtypeStruct((T, D), BF16), jax.ShapeDtypeStruct((T, D), F32),
                 jax.ShapeDtypeStruct((T, D), BF16)]
    if prompt:
        out_specs += [pl.BlockSpec((D, tm), lambda i: (0, i)), small,
                      pl.BlockSpec((H, tm, LANE), lambda i: (0, i, 0))]
        out_shape += [jax.ShapeDtypeStruct((D, T), BF16), jax.ShapeDtypeStruct((T, LANE), F32),
                      jax.ShapeDtypeStruct((H, T, LANE), F32)]
        scratch = [pltpu.VMEM((1, LANE), F32)]
    else:
        out_specs += [blk(0), small]
        out_shape += [jax.ShapeDtypeStruct((T, D), BF16), jax.ShapeDtypeStruct((T, LANE), F32)]
        scratch = []
    return pl.pallas_call(
        functools.partial(_qkv_post_kernel, prompt=prompt),
        grid=(T // tm,),
        in_specs=[blk(0), blk(1), blk(2), small, vec, vec, vec],
        out_specs=out_specs,
        out_shape=out_shape,
        scratch_shapes=scratch,
        compiler_params=_cparams(("arbitrary",)),
        name="qkv_post",
    )(proj, proj, proj, projf, b_f, q_g, k_g)


def _flash_kernel(q_ref, k_ref, vt_ref, b_ref, o_ref, s_ref, m_ref, l_ref, acc_ref, *, tq, hps):
    qi = pl.program_id(1)
    m_ref[...] = jnp.full_like(m_ref, NEG_INF)
    l_ref[...] = jnp.zeros_like(l_ref)
    acc_ref[...] = jnp.zeros_like(acc_ref)
    heads = [slice(g * HEAD_DIM, (g + 1) * HEAD_DIM) for g in range(hps)]

    def scores(g, i):
        k0 = pl.multiple_of(i * tq, tq)
        bias = b_ref[g, pl.ds(k0, tq), :]
        return (_dot_nt(k_ref[pl.ds(k0, tq), heads[g]], q_ref[:, heads[g]])
                + jnp.concatenate([bias] * (tq // LANE), axis=1))

    def update(g, s, i):
        k0 = pl.multiple_of(i * tq, tq)
        m_prev = m_ref[g]
        m_new = jnp.maximum(m_prev, jnp.max(s, axis=0, keepdims=True))
        alpha = jnp.exp2(m_prev - m_new)
        p = jnp.exp2(s - m_new)
        l_ref[g] = alpha * l_ref[g] + jnp.sum(p, axis=0, keepdims=True)
        acc_ref[g] = alpha * acc_ref[g] + jnp.dot(vt_ref[heads[g], pl.ds(k0, tq)], p.astype(BF16),
                                                 preferred_element_type=F32)
        m_ref[g] = m_new

    for g in range(hps):
        s_ref[g, 0] = scores(g, 0)

    def body(i, carry):
        nxt = [scores(g, i + 1) for g in range(hps)]
        for g in range(hps):
            update(g, s_ref[g, i % 2], i)
        for g in range(hps):
            s_ref[g, (i + 1) % 2] = nxt[g]
        return carry

    lax.fori_loop(0, qi, body, 0)
    for g in range(hps):
        s = s_ref[g, qi % 2]
        key = lax.broadcasted_iota(jnp.int32, s.shape, 0)
        qry = lax.broadcasted_iota(jnp.int32, s.shape, 1)
        update(g, jnp.where(key <= qry, s, NEG_INF), qi)
        o_ref[:, heads[g]] = (acc_ref[g] / l_ref[g]).T.astype(o_ref.dtype)


def _flash(q, k, vt, bias, *, tq=512, hps=1):
    T, D = q.shape
    H = D // HEAD_DIM
    tq = min(tq, T)
    assert T % tq == 0 and H % hps == 0
    W = hps * HEAD_DIM
    return pl.pallas_call(
        functools.partial(_flash_kernel, tq=tq, hps=hps),
        grid=(H // hps, T // tq),
        in_specs=[
            pl.BlockSpec((tq, W), lambda h, i: (i, h)),
            pl.BlockSpec((T, W), lambda h, i: (0, h)),
            pl.BlockSpec((W, T), lambda h, i: (h, 0)),
            pl.BlockSpec((hps, T, LANE), lambda h, i: (h, 0, 0)),
        ],
        out_specs=pl.BlockSpec((tq, W), lambda h, i: (i, h)),
        out_shape=jax.ShapeDtypeStruct((T, D), BF16),
        scratch_shapes=[pltpu.VMEM((hps, 2, tq, tq), F32), pltpu.VMEM((hps, 1, tq), F32),
                        pltpu.VMEM((hps, 1, tq), F32), pltpu.VMEM((hps, HEAD_DIM, tq), F32)],
        compiler_params=_cparams(("parallel", "arbitrary")),
        name="flash",
    )(q, k, vt, bias)


def _decode_kernel(pt_ref, q_ref, *refs, pps, n_steps, n_heads):
    kc = refs[0:pps]
    vc = refs[pps:2 * pps]
    lc = refs[2 * pps:3 * pps]
    kn_ref, vn_ref, ln_ref, o_ref, m_ref, l_ref, acc_ref, carry_ref, hm_ref = refs[3 * pps:]
    p = pl.program_id(1)
    width = hm_ref.shape[1]

    @pl.when(p == 0)
    def _():
        m_ref[...] = jnp.full_like(m_ref, NEG_INF)
        l_ref[...] = jnp.zeros_like(l_ref)
        acc_ref[...] = jnp.zeros_like(acc_ref)
        carry_ref[...] = jnp.zeros_like(carry_ref)
        row = lax.broadcasted_iota(jnp.int32, hm_ref.shape, 0)
        col = lax.broadcasted_iota(jnp.int32, hm_ref.shape, 1)
        hm_ref[...] = jnp.where((col % n_heads) == (row % n_heads), 0.0, NEG_INF)

    def page_bias(lf_ref):
        x = jnp.broadcast_to(lf_ref[...], carry_ref.shape)
        lane = lax.broadcasted_iota(jnp.int32, x.shape, 1)
        tot = x
        sh = n_heads
        while sh < width:
            x = x + jnp.where(lane >= sh, pltpu.roll(x, sh, axis=1), 0.0)
            tot = tot + pltpu.roll(tot, sh, axis=1)
            sh *= 2
        c = x + carry_ref[...]
        carry_ref[...] = carry_ref[...] + tot
        return c[0:1, :] * (-LOG2E)

    def attend(ks, vs, biases, causal):
        q = q_ref[...]
        ss = []
        for kb, b in zip(ks, biases):
            s = _dot_nt(q, kb) + (hm_ref[...] + b)
            if causal:
                row = lax.broadcasted_iota(jnp.int32, s.shape, 0)
                col = lax.broadcasted_iota(jnp.int32, s.shape, 1)
                s = jnp.where(col // n_heads <= row // n_heads, s, NEG_INF)
            ss.append(s)
        m_prev = m_ref[...]
        m_new = m_prev
        for s in ss:
            m_new = jnp.maximum(m_new, jnp.max(s, axis=-1, keepdims=True))
        alpha = jnp.exp2(m_prev - m_new)
        l = alpha * l_ref[...]
        acc = alpha * acc_ref[...]
        for s, vb in zip(ss, vs):
            pr = jnp.exp2(s - m_new)
            l = l + jnp.sum(pr, axis=-1, keepdims=True)
            acc = acc + jnp.dot(pr.astype(BF16), vb, preferred_element_type=F32)
        m_ref[...] = m_new
        l_ref[...] = l
        acc_ref[...] = acc

    @pl.when(p < n_steps)
    def _():
        biases = [page_bias(r) for r in lc]
        attend([r[...].astype(BF16) for r in kc], [r[...].astype(BF16) for r in vc], biases, False)

    @pl.when(p == n_steps)
    def _():
        attend([kn_ref[...]], [vn_ref[...]], [page_bias(ln_ref)], True)
        o_ref[...] = acc_ref[...] / l_ref[...]


def _decode(page_ids, q2, cache_k, cache_v, cache_lf, k_new, v_new, lf_new, *, pps=8):
    Bd, P = page_ids.shape
    R = q2.shape[1]
    W = cache_k.shape[1]
    H = W // PAGE_SIZE
    assert P % pps == 0
    n_steps = P // pps
    last = n_steps - 1

    def page_map(i):
        return lambda b, p, pt: (pt[b, jnp.minimum(p, last) * pps + i], 0, 0)

    seq_map = lambda b, p, pt: (b, 0, 0)
    in_specs = [pl.BlockSpec((None, R, HEAD_DIM), seq_map)]
    in_specs += [pl.BlockSpec((None, W, HEAD_DIM), page_map(i)) for i in range(pps)]
    in_specs += [pl.BlockSpec((None, W, HEAD_DIM), page_map(i)) for i in range(pps)]
    in_specs += [pl.BlockSpec((None, 1, W), page_map(i)) for i in range(pps)]
    in_specs += [pl.BlockSpec((None, W, HEAD_DIM), seq_map), pl.BlockSpec((None, W, HEAD_DIM), seq_map),
                 pl.BlockSpec((None, 1, W), seq_map)]
    grid_spec = pltpu.PrefetchScalarGridSpec(
        num_scalar_prefetch=1,
        grid=(Bd, n_steps + 1),
        in_specs=in_specs,
        out_specs=pl.BlockSpec((None, R, HEAD_DIM), seq_map),
        scratch_shapes=[pltpu.VMEM((R, 1), F32), pltpu.VMEM((R, 1), F32), pltpu.VMEM((R, HEAD_DIM), F32),
                        pltpu.VMEM((SUBLANE, W), F32), pltpu.VMEM((R, W), F32)],
    )
    return pl.pallas_call(
        functools.partial(_decode_kernel, pps=pps, n_steps=n_steps, n_heads=H),
        grid_spec=grid_spec,
        out_shape=jax.ShapeDtypeStruct((Bd, R, HEAD_DIM), F32),
        compiler_params=_cparams(("parallel", "arbitrary")),
        name="decode",
    )(page_ids, q2, *([cache_k] * pps), *([cache_v] * pps), *([cache_lf] * pps), k_new, v_new, lf_new)


def _cmpx(v, i, j):
    hi = jnp.maximum(v[i], v[j])
    lo = jnp.minimum(v[i], v[j])
    v[i], v[j] = hi, lo


def _bitonic_merge_desc(v):
    n = len(v)
    j = n // 2
    while j >= 1:
        for i in range(n):
            if i & j == 0:
                _cmpx(v, i, i | j)
        j //= 2
    return v


def _sort_desc(v):
    v = list(v)
    n = len(v)
    k = 2
    while k <= n:
        j = k // 2
        while j >= 1:
            for i in range(n):
                l = i ^ j
                if l > i:
                    if (i & k) == 0:
                        _cmpx(v, i, l)
                    else:
                        _cmpx(v, l, i)
            j //= 2
        k *= 2
    return v


def _top16_sorted(v):
    groups = [_sort_desc(v[g:g + TOPK]) for g in range(0, len(v), TOPK)]
    while len(groups) > 1:
        nxt = []
        for g in range(0, len(groups), 2):
            a, b = groups[g], groups[g + 1]
            c = [jnp.maximum(a[i], b[TOPK - 1 - i]) for i in range(TOPK)]
            nxt.append(_bitonic_merge_desc(c))
        groups = nxt
    return groups[0]


def _peer_route_kernel(q_ref, kb_ref, thr_ref, e1_ref, s2_ref, e2_ref, *, n_heads):
    nrow = n_heads * N_KEYS
    tb = q_ref.shape[0]
    tops = []
    per_head = []
    for p in range(2):
        qp = q_ref[:, p * nrow:(p + 1) * nrow]
        res = _dot_nt(kb_ref[p], qp)
        srt = res[0:nrow].reshape(N_KEYS, n_heads, tb)
        per_head.append(res[nrow:2 * nrow].reshape(n_heads, N_KEYS, tb))
        tops.append(_top16_sorted([srt[k] for k in range(N_KEYS)]))
    v1, v2 = tops
    pairs = [(a, b) for a in range(TOPK) for b in range(TOPK) if (a + 1) * (b + 1) <= TOPK]
    cand = {ab: v1[ab[0]] + v2[ab[1]] for ab in pairs}
    pad = jnp.full_like(v1[0], -BIG)
    top = _top16_sorted([cand[ab] for ab in pairs] + [pad] * (4 * TOPK - len(pairs)))
    tau = top[TOPK - 1]
    z = jnp.ones_like(tau)
    for k in range(1, TOPK):
        z = z + jnp.exp(top[k] - top[0])
    inv_z = 1.0 / z
    thr_rank = []
    for a in range(TOPK):
        t = jnp.full_like(tau, BIG)
        for b in range(TOPK):
            if (a, b) in cand:
                t = jnp.minimum(t, jnp.where(cand[(a, b)] >= tau, v2[b], BIG))
        thr_rank.append(t)
    s1, s2 = per_head
    for h in range(n_heads):
        row = lambda x: x[h:h + 1, :]
        s1h = s1[h]
        thr = jnp.full_like(s1h, BIG)
        for a in range(TOPK):
            thr = jnp.minimum(thr, jnp.where(s1h >= row(v1[a]), row(thr_rank[a]), BIG))
        thr_ref[h, 0] = thr
        e1_ref[h, 0] = jnp.exp(s1h - row(v1[0]))
        s2_ref[h, 0] = s2[h]
        e2_ref[h, 0] = jnp.exp(s2[h] - row(v2[0])) * row(inv_z)


def _peer_route(q, kb, *, n_heads):
    N = q.shape[0]
    tb = LANE
    assert N % tb == 0
    big = lambda: pl.BlockSpec((n_heads, 1, N_KEYS, tb), lambda i: (0, i, 0, 0))
    shp = jax.ShapeDtypeStruct((n_heads, N // tb, N_KEYS, tb), F32)
    return pl.pallas_call(
        functools.partial(_peer_route_kernel, n_heads=n_heads),
        grid=(N // tb,),
        in_specs=[pl.BlockSpec((tb, q.shape[1]), lambda i: (i, 0)),
                  pl.BlockSpec(kb.shape, lambda i: (0, 0, 0))],
        out_specs=[big(), big(), big(), big()],
        out_shape=[shp, shp, shp, shp],
        compiler_params=_cparams(("parallel",)),
        name="peer_route",
    )(q, kb)


def _peer_dense_kernel(xn_ref, u_ref, v_ref, thr_ref, e1_ref, s2_ref, e2_ref, res_ref,
                       out_ref, w_ref, *, n_heads, rows_per_step):
    e = pl.program_id(1)
    n_tok_blocks = thr_ref.shape[1]

    @pl.when(e == 0)
    def _():
        out_ref[...] = res_ref[...]

    act = _dot_nt(u_ref[...], xn_ref[...])
    for r in range(rows_per_step):
        i1 = e * rows_per_step + r
        rows = slice(r * N_KEYS, (r + 1) * N_KEYS)
        for c in range(n_tok_blocks):
            cols = slice(c * LANE, (c + 1) * LANE)
            acc = None
            for h in range(n_heads):
                sel = jnp.where(s2_ref[h, c] >= thr_ref[h, c, pl.ds(i1, 1), :], e2_ref[h, c], 0.0)
                term = sel * e1_ref[h, c, pl.ds(i1, 1), :]
                acc = term if acc is None else acc + term
            w_ref[rows, cols] = (_gelu(act[rows, cols]) * acc).astype(BF16)
    out_ref[...] += _dot_tn(w_ref[...], v_ref[...])


def _peer_dense(xn, u, v, thr, e1, s2, e2, resid, *, tb=512, te=1024):
    N, D = xn.shape
    E = u.shape[0]
    n_heads = thr.shape[0]
    tb = min(tb, N)
    assert N % tb == 0 and E % te == 0 and te % N_KEYS == 0 and tb % LANE == 0
    rps = te // N_KEYS
    once = pl.Buffered(1)
    tok = lambda: pl.BlockSpec((n_heads, tb // LANE, N_KEYS, LANE), lambda i, e: (0, i, 0, 0),
                               pipeline_mode=once)
    return pl.pallas_call(
        functools.partial(_peer_dense_kernel, n_heads=n_heads, rows_per_step=rps),
        grid=(N // tb, E // te),
        in_specs=[
            pl.BlockSpec((tb, D), lambda i, e: (i, 0), pipeline_mode=once),
            pl.BlockSpec((te, D), lambda i, e: (e, 0)),
            pl.BlockSpec((te, D), lambda i, e: (e, 0)),
            tok(), tok(), tok(), tok(),
            pl.BlockSpec((tb, D), lambda i, e: (i, 0), pipeline_mode=once),
        ],
        out_specs=pl.BlockSpec((tb, D), lambda i, e: (i, 0)),
        out_shape=jax.ShapeDtypeStruct((N, D), F32),
        scratch_shapes=[pltpu.VMEM((te, tb), BF16)],
        compiler_params=_cparams(("parallel", "arbitrary")),
        name="peer_dense",
    )(xn, u, v, thr, e1, s2, e2, resid)


def _peer_weights(w_q, sub_keys):
    D = w_q.shape[0]
    H = sub_keys.shape[0]
    half = sub_keys.shape[-1]
    wq = w_q.reshape(D, H, 2, half).transpose(0, 2, 1, 3).reshape(D, 2 * H * half).astype(BF16)
    eye = jnp.eye(H, dtype=sub_keys.dtype)
    sk = sub_keys.transpose(1, 0, 2, 3)
    bd = sk[:, :, :, None, :] * eye[None, :, None, :, None]
    hk = bd.reshape(2, H * N_KEYS, H * half)
    kh = bd.transpose(0, 2, 1, 3, 4).reshape(2, N_KEYS * H, H * half)
    return wq, jnp.concatenate([kh, hk], axis=1).astype(BF16)


def _peer(x, gain, wq, kb, u, v, *, n_heads):
    xn = _rms_norm(x, gain)
    q = _mm(xn, wq, out_dtype=BF16)
    thr, e1, s2, e2 = _peer_route(q, kb, n_heads=n_heads)
    return _peer_dense(xn, u, v, thr, e1, s2, e2, x)


def _pad_lanes(a, width=LANE):
    return jnp.pad(a, [(0, 0)] * (a.ndim - 1) + [(0, width - a.shape[-1])])


def kernel(x_prompt, x_sample, state_conv, state_h, cache_k, cache_v, cache_logf, page_table,
           norm_mix, norm_ffn,
           rec_w_in, rec_conv_w, rec_conv_b, rec_w_a, rec_b_a, rec_w_i, rec_b_i, rec_lambda, rec_w_out,
           att_w_in, att_b_f, att_q_norm, att_k_norm, att_w_out,
           peer_w_q, peer_sub_keys, peer_u, peer_v):
    Bp, S, D = x_prompt.shape
    Bd, Sd, _ = x_sample.shape
    assert Bp == 1
    H = D // HEAD_DIM
    PH = peer_sub_keys.shape[1]
    depth = norm_mix.shape[0]
    n_att, n_pool = cache_k.shape[0], cache_k.shape[1]
    page_w = PAGE_SIZE * H

    xp = x_prompt.reshape(S, D)
    xs = x_sample.reshape(Bd * Sd, D)
    row = lambda a: a.reshape(1, -1).astype(F32)
    cache_k2 = cache_k.reshape(n_att * n_pool, page_w, HEAD_DIM)
    cache_v2 = cache_v.reshape(n_att * n_pool, page_w, HEAD_DIM)
    cache_lf2 = cache_logf.reshape(n_att * n_pool, 1, page_w)

    pc, ph, pk, pv, plf = [], [], [], [], []
    sc, sh, sk, sv, slf = [], [], [], [], []
    for i in range(depth):
        j = i // 2
        if i % 2 == 0:
            w_in = _to_bf16(rec_w_in, j)
            w_out = _to_bf16(rec_w_out, j)
            wa = rec_w_a[j].astype(BF16)
            wi = rec_w_i[j].astype(BF16)
            cw, cb = rec_conv_w[j], row(rec_conv_b[j])
            ba, bi, lam = row(rec_b_a[j]), row(rec_b_i[j]), row(rec_lambda[j])
            proj_p = _mm(xp, w_in, gain=norm_mix[i])
            yp, hl_p = _rglru_prompt(proj_p, jnp.zeros((SUBLANE, D), F32), jnp.zeros((1, D), F32),
                                     cw, cb, wa, wi, ba, bi, lam)
            pc.append(proj_p[S - (CONV_W - 1):, D:].reshape(1, CONV_W - 1, D))
            ph.append(hl_p)
            xp = _mm(yp, w_out, residual=xp)
            proj_s = _mm(xs, w_in, gain=norm_mix[i]).reshape(Bd, Sd, 2 * D)
            ext = jnp.concatenate([state_conv[j], proj_s[..., D:]], axis=1)
            ys_tm, hl_s = _rglru_sample(proj_s[..., :D].transpose(1, 0, 2), ext.transpose(1, 0, 2),
                                        state_h[j], cw, cb, wa, wi, ba, bi, lam)
            sc.append(ext[:, Sd:])
            sh.append(hl_s)
            xs = _mm(ys_tm.transpose(1, 0, 2).reshape(Bd * Sd, D), w_out, residual=xs)
        else:
            w_qkv = att_w_in[j][:, :3 * D].astype(BF16)
            w_f = _pad_lanes(att_w_in[j][:, 3 * D:]).astype(BF16)
            w_out = _to_bf16(att_w_out, j)
            b_f = _pad_lanes(row(att_b_f[j]))
            qg, kg = row(att_q_norm[j]), row(att_k_norm[j])
            proj = _mm(xp, w_qkv, gain=norm_mix[i])
            projf = _mm(xp, w_f, gain=norm_mix[i])
            qb, kf, kbf, vt, lf, bias = _qkv_post(proj, projf, b_f, qg, kg, prompt=True)
            op = _flash(qb, kbf, vt, bias)
            pk.append(kf.reshape(1, S, H, HEAD_DIM))
            pv.append(proj[:, 2 * D:].reshape(1, S, H, HEAD_DIM))
            plf.append(lf[:, :H].reshape(1, S, H))
            xp = _mm(op, w_out, residual=xp)
            n_s = Bd * Sd
            proj = _mm(xs, w_qkv, gain=norm_mix[i])
            projf = _mm(xs, w_f, gain=norm_mix[i])
            qb, kf, kbf, vbf, lf = _qkv_post(proj, projf, b_f, qg, kg, prompt=False)
            lf_s = lf[:, :H].reshape(Bd, Sd, H)
            as_page = lambda a: jnp.pad(a.reshape(Bd, Sd * H, -1),
                                        ((0, 0), (0, page_w - Sd * H), (0, 0)))
            o = _decode(page_table + j * n_pool, qb.reshape(Bd, Sd * H, HEAD_DIM),
                        cache_k2, cache_v2, cache_lf2,
                        as_page(kbf), as_page(vbf), as_page(lf_s).reshape(Bd, 1, page_w))
            sk.append(kf.reshape(Bd, Sd, H, HEAD_DIM))
            sv.append(proj[:, 2 * D:].reshape(Bd, Sd, H, HEAD_DIM))
            slf.append(lf_s)
            xs = _mm(o.reshape(n_s, D), w_out, residual=xs)
        wq, kb = _peer_weights(peer_w_q[i], peer_sub_keys[i])
        u = _to_bf16(peer_u, i)
        v = _to_bf16(peer_v, i)
        xp = _peer(xp, norm_ffn[i], wq, kb, u, v, n_heads=PH)
        xs = _peer(xs, norm_ffn[i], wq, kb, u, v, n_heads=PH)

    return (xp.reshape(1, S, D), xs.reshape(Bd, Sd, D),
            jnp.stack(pc), jnp.stack(ph), jnp.stack(pk), jnp.stack(pv), jnp.stack(plf),
            jnp.stack(sc), jnp.stack(sh), jnp.stack(sk), jnp.stack(sv), jnp.stack(slf))
```

```python
import functools
import math

import jax
import jax.numpy as jnp
from jax import lax
from jax.experimental import pallas as pl
from jax.experimental.pallas import tpu as pltpu

F32 = jnp.float32
BF16 = jnp.bfloat16

EPS = 1e-6
NEG_INF = -1e30
BIG = 3.0e38
LOG2E = math.log2(math.e)
RG_C = 8.0
CONV_W = 4
HEAD_DIM = 128
PAGE_SIZE = 128
TOPK = 16
N_KEYS = 128
LANE = 128
SUBLANE = 8
VMEM_LIMIT = 56 * 1024 * 1024


def _cparams(sem, flags=None):
    return pltpu.CompilerParams(dimension_semantics=sem, vmem_limit_bytes=VMEM_LIMIT, flags=flags)


def _largest_tile(n, target, unit):
    t = min(target, n) // unit * unit
    while n % t:
        t -= unit
    return t


def _gelu(x):
    c = math.sqrt(2.0 / math.pi)
    return 0.5 * x * (1.0 + jnp.tanh(x * (c + (0.044715 * c) * (x * x))))


def _log_sigmoid(z):
    return jnp.minimum(z, 0.0) - jnp.log1p(jnp.exp(-jnp.abs(z)))


def _sigmoid(z):
    return 1.0 / (1.0 + jnp.exp(-z))


def _dot_nt(a, b):
    return lax.dot_general(a, b, (((1,), (1,)), ((), ())), preferred_element_type=F32)


def _dot_tn(a, b):
    return lax.dot_general(a, b, (((0,), (0,)), ((), ())), preferred_element_type=F32)


def _mm_kernel(*refs, norm, residual):
    it = iter(refs)
    x_ref = next(it)
    g_ref = next(it) if norm else None
    w_ref = next(it)
    res_ref = next(it) if residual else None
    out_ref = next(it)
    xn_ref = next(it) if norm else None

    if norm:
        @pl.when(pl.program_id(1) == 0)
        def _():
            xf = x_ref[...].astype(F32)
            y = xf * lax.rsqrt(jnp.mean(xf * xf, axis=-1, keepdims=True) + EPS)
            xn_ref[...] = (y * g_ref[...]).astype(BF16)
        lhs = xn_ref[...]
    else:
        lhs = x_ref[...].astype(BF16)
    acc = jnp.dot(lhs, w_ref[...], preferred_element_type=F32)
    if residual:
        acc = acc + res_ref[...]
    out_ref[...] = acc.astype(out_ref.dtype)


def _mm(x, w, *, gain=None, residual=None, out_dtype=F32, tm=1024, tn=1024):
    M, K = x.shape
    N = w.shape[1]
    tm = _largest_tile(M, tm, SUBLANE)
    tn = _largest_tile(N, tn, LANE)
    norm = gain is not None
    in_specs = [pl.BlockSpec((tm, K), lambda i, j: (i, 0))]
    args = [x]
    if norm:
        in_specs.append(pl.BlockSpec((1, K), lambda i, j: (0, 0)))
        args.append(gain.reshape(1, K).astype(F32))
    in_specs.append(pl.BlockSpec((K, tn), lambda i, j: (0, j)))
    args.append(w)
    if residual is not None:
        in_specs.append(pl.BlockSpec((tm, tn), lambda i, j: (i, j)))
        args.append(residual)
    scratch = [pltpu.VMEM((tm, K), BF16)] if norm else []
    return pl.pallas_call(
        functools.partial(_mm_kernel, norm=norm, residual=residual is not None),
        grid=(M // tm, N // tn),
        in_specs=in_specs,
        out_specs=pl.BlockSpec((tm, tn), lambda i, j: (i, j)),
        out_shape=jax.ShapeDtypeStruct((M, N), out_dtype),
        scratch_shapes=scratch,
        compiler_params=_cparams(("parallel", "arbitrary")),
        name="mm",
    )(*args)


def _cast_kernel(x_ref, o_ref):
    o_ref[...] = x_ref[...].astype(o_ref.dtype)


def _to_bf16(w, layer, *, rows=1024):
    _, M, K = w.shape
    rows = min(rows, M)
    assert M % rows == 0
    return pl.pallas_call(
        _cast_kernel,
        grid=(M // rows,),
        in_specs=[pl.BlockSpec((None, rows, K), lambda i: (layer, i, 0))],
        out_specs=pl.BlockSpec((rows, K), lambda i: (i, 0)),
        out_shape=jax.ShapeDtypeStruct((M, K), BF16),
        compiler_params=_cparams(("parallel",)),
        name="to_bf16",
    )(w)


def _norm_kernel(x_ref, g_ref, o_ref):
    xf = x_ref[...].astype(F32)
    y = xf * lax.rsqrt(jnp.mean(xf * xf, axis=-1, keepdims=True) + EPS)
    o_ref[...] = (y * g_ref[...]).astype(o_ref.dtype)


def _rms_norm(x, gain, *, tm=512):
    M, K = x.shape
    tm = min(tm, M)
    assert M % tm == 0
    return pl.pallas_call(
        _norm_kernel,
        grid=(M // tm,),
        in_specs=[pl.BlockSpec((tm, K), lambda i: (i, 0)), pl.BlockSpec((1, K), lambda i: (0, 0))],
        out_specs=pl.BlockSpec((tm, K), lambda i: (i, 0)),
        out_shape=jax.ShapeDtypeStruct((M, K), BF16),
        compiler_params=_cparams(("parallel",)),
        name="rms_norm",
    )(x, gain.reshape(1, K).astype(F32))


def _rg_coeffs(xc, w_a, w_i, b_a, b_i, lam):
    xcb = xc.astype(BF16)
    r = _sigmoid(jnp.dot(xcb, w_a, preferred_element_type=F32) + b_a)
    i = _sigmoid(jnp.dot(xcb, w_i, preferred_element_type=F32) + b_i)
    log_a = RG_C * r * _log_sigmoid(lam)
    a = jnp.exp(log_a)
    th = jnp.tanh(log_a)
    b = jnp.sqrt(-2.0 * th / (1.0 - th)) * (i * xc)
    return a, b


def _scan_rows(a, b):
    n = a.shape[0]
    row = lax.broadcasted_iota(jnp.int32, a.shape, 0)
    s = 1
    while s < n:
        a_sh = pltpu.roll(a, s, axis=0)
        b_sh = pltpu.roll(b, s, axis=0)
        valid = row >= s
        b = jnp.where(valid, a * b_sh + b, b)
        a = jnp.where(valid, a * a_sh, a)
        s *= 2
    return a, b


def _cumsum_rows(x):
    n = x.shape[0]
    row = lax.broadcasted_iota(jnp.int32, x.shape, 0)
    s = 1
    while s < n:
        x = x + jnp.where(row >= s, pltpu.roll(x, s, axis=0), 0.0)
        s *= 2
    return x


def _rglru_prompt_kernel(gate_ref, xr_ref, cw_ref, cb_ref, wa_ref, wi_ref, ba_ref, bi_ref,
                         lam_ref, init_ref, h0_ref, y_ref, hl_ref, ext_ref, h_ref, *, tt):
    t = pl.program_id(1)

    @pl.when(t == 0)
    def _():
        ext_ref[0:SUBLANE, :] = init_ref[...]
        h_ref[...] = h0_ref[...]

    xr = xr_ref[...]
    ext_ref[SUBLANE:SUBLANE + tt, :] = xr
    xc = cb_ref[...] + xr * cw_ref[CONV_W - 1:CONV_W, :]
    for k in range(CONV_W - 1):
        xc = xc + ext_ref[pl.ds(SUBLANE - (CONV_W - 1) + k, tt), :] * cw_ref[k:k + 1, :]
    ext_ref[0:SUBLANE, :] = xr[tt - SUBLANE:tt, :]

    a, b = _rg_coeffs(xc, wa_ref[...], wi_ref[...], ba_ref[...], bi_ref[...], lam_ref[...])
    A, B = _scan_rows(a, b)
    h = A * h_ref[...] + B
    h_ref[...] = h[tt - 1:tt, :]
    hl_ref[...] = h[tt - 1:tt, :]
    y_ref[...] = (h * _gelu(gate_ref[...])).astype(y_ref.dtype)


def _rglru_prompt(proj, conv_init, h0, cw, cb, wa, wi, ba, bi, lam, *, tt=256):
    T = proj.shape[0]
    D = proj.shape[1] // 2
    nb, C = wa.shape[0], wa.shape[1]
    tt = min(tt, T)
    assert T % tt == 0 and tt % SUBLANE == 0
    vec = lambda: pl.BlockSpec((1, C), lambda n, t: (0, n))
    y, hl = pl.pallas_call(
        functools.partial(_rglru_prompt_kernel, tt=tt),
        grid=(nb, T // tt),
        in_specs=[
            pl.BlockSpec((tt, C), lambda n, t: (t, n)),
            pl.BlockSpec((tt, C), lambda n, t: (t, nb + n)),
            pl.BlockSpec((CONV_W, C), lambda n, t: (0, n)),
            vec(),
            pl.BlockSpec((None, C, C), lambda n, t: (n, 0, 0)),
            pl.BlockSpec((None, C, C), lambda n, t: (n, 0, 0)),
            vec(), vec(), vec(),
            pl.BlockSpec((SUBLANE, C), lambda n, t: (0, n)),
            vec(),
        ],
        out_specs=[pl.BlockSpec((tt, C), lambda n, t: (t, n)),
                   pl.BlockSpec((1, C), lambda n, t: (0, n))],
        out_shape=[jax.ShapeDtypeStruct((T, D), BF16), jax.ShapeDtypeStruct((1, D), F32)],
        scratch_shapes=[pltpu.VMEM((tt + SUBLANE, C), F32), pltpu.VMEM((1, C), F32)],
        compiler_params=_cparams(("parallel", "arbitrary")),
        name="rglru_prompt",
    )(proj, proj, cw, cb, wa, wi, ba, bi, lam, conv_init, h0)
    return y, hl


def _rglru_sample_kernel(gate_ref, ext_ref, cw_ref, cb_ref, wa_ref, wi_ref, ba_ref, bi_ref,
                         lam_ref, h0_ref, y_ref, hl_ref, *, steps):
    h = h0_ref[...]
    for t in range(steps):
        xc = cb_ref[...]
        for k in range(CONV_W):
            xc = xc + ext_ref[t + k] * cw_ref[k:k + 1, :]
        a, b = _rg_coeffs(xc, wa_ref[...], wi_ref[...], ba_ref[...], bi_ref[...], lam_ref[...])
        h = a * h + b
        y_ref[t] = (h * _gelu(gate_ref[t])).astype(y_ref.dtype)
    hl_ref[...] = h


def _rglru_sample(gate_tm, ext_tm, h0, cw, cb, wa, wi, ba, bi, lam):
    S, B, D = gate_tm.shape
    nb, C = wa.shape[0], wa.shape[1]
    vec = lambda: pl.BlockSpec((1, C), lambda n: (0, n))
    return pl.pallas_call(
        functools.partial(_rglru_sample_kernel, steps=S),
        grid=(nb,),
        in_specs=[
            pl.BlockSpec((S, B, C), lambda n: (0, 0, n)),
            pl.BlockSpec((S + CONV_W - 1, B, C), lambda n: (0, 0, n)),
            pl.BlockSpec((CONV_W, C), lambda n: (0, n)),
            vec(),
            pl.BlockSpec((None, C, C), lambda n: (n, 0, 0)),
            pl.BlockSpec((None, C, C), lambda n: (n, 0, 0)),
            vec(), vec(), vec(),
            pl.BlockSpec((B, C), lambda n: (0, n)),
        ],
        out_specs=[pl.BlockSpec((S, B, C), lambda n: (0, 0, n)),
                   pl.BlockSpec((B, C), lambda n: (0, n))],
        out_shape=[jax.ShapeDtypeStruct((S, B, D), BF16), jax.ShapeDtypeStruct((B, D), F32)],
        compiler_params=_cparams(("parallel",)),
        name="rglru_sample",
    )(gate_tm, ext_tm, cw, cb, wa, wi, ba, bi, lam, h0)


def _qkv_post_kernel(q_ref, k_ref, v_ref, f_ref, bf_ref, qg_ref, kg_ref, *refs, prompt):
    if prompt:
        qb_ref, kf_ref, kb_ref, vt_ref, lf_ref, bias_ref, carry_ref = refs
    else:
        qb_ref, kf_ref, kb_ref, vb_ref, lf_ref = refs
    n_heads = q_ref.shape[1] // HEAD_DIM
    q_scale = HEAD_DIM ** -0.5 * LOG2E
    for h in range(n_heads):
        sl = slice(h * HEAD_DIM, (h + 1) * HEAD_DIM)
        qh = q_ref[:, sl]
        qn = qh * lax.rsqrt(jnp.mean(qh * qh, axis=-1, keepdims=True) + EPS) * qg_ref[...]
        qb_ref[:, sl] = (qn * q_scale).astype(BF16)
        kh = k_ref[:, sl]
        kn = kh * lax.rsqrt(jnp.mean(kh * kh, axis=-1, keepdims=True) + EPS) * kg_ref[...]
        kf_ref[:, sl] = kn
        kb_ref[:, sl] = kn.astype(BF16)
        if prompt:
            vt_ref[sl, :] = v_ref[:, sl].T.astype(BF16)
    if not prompt:
        vb_ref[...] = v_ref[...].astype(BF16)
    lf = _log_sigmoid(f_ref[...] + bf_ref[...])
    lf_ref[...] = lf
    if prompt:
        @pl.when(pl.program_id(0) == 0)
        def _():
            carry_ref[...] = jnp.zeros_like(carry_ref)

        c = _cumsum_rows(lf) + carry_ref[...]
        carry_ref[...] = c[c.shape[0] - 1:, :]
        nb = c * (-LOG2E)
        for h in range(n_heads):
            bias_ref[h] = jnp.broadcast_to(nb[:, h:h + 1], (nb.shape[0], LANE))


def _qkv_post(proj, projf, b_f, q_g, k_g, *, prompt, tm=256):
    T = proj.shape[0]
    D = proj.shape[1] // 3
    H = D // HEAD_DIM
    tm = min(tm, T)
    assert T % tm == 0
    blk = lambda j: pl.BlockSpec((tm, D), lambda i: (i, j))
    small = pl.BlockSpec((tm, LANE), lambda i: (i, 0))
    vec = pl.BlockSpec((1, LANE), lambda i: (0, 0))
    out_specs = [blk(0), blk(0), blk(0)]
    out_shape = [jax.ShapeDtypeStruct((T, D), BF16), jax.ShapeDtypeStruct((T, D), F32),
                 jax.ShapeDtypeStruct((T, D), BF16)]
    if prompt:
        out_specs += [pl.BlockSpec((D, tm), lambda i: (0, i)), small,
                      pl.BlockSpec((H, tm, LANE), lambda i: (0, i, 0))]
        out_shape += [jax.ShapeDtypeStruct((D, T), BF16), jax.ShapeDtypeStruct((T, LANE), F32),
                      jax.ShapeDtypeStruct((H, T, LANE), F32)]
        scratch = [pltpu.VMEM((1, LANE), F32)]
    else:
        out_specs += [blk(0), small]
        out_shape += [jax.ShapeDtypeStruct((T, D), BF16), jax.ShapeDtypeStruct((T, LANE), F32)]
        scratch = []
    return pl.pallas_call(
        functools.partial(_qkv_post_kernel, prompt=prompt),
        grid=(T // tm,),
        in_specs=[blk(0), blk(1), blk(2), small, vec, vec, vec],
        out_specs=out_specs,
        out_shape=out_shape,
        scratch_shapes=scratch,
        compiler_params=_cparams(("arbitrary",)),
        name="qkv_post",
    )(proj, proj, proj, projf, b_f, q_g, k_g)


def _flash_kernel(q_ref, k_ref, vt_ref, b_ref, o_ref, s_ref, m_ref, l_ref, acc_ref, *, tq, hps):
    qi = pl.program_id(1)
    m_ref[...] = jnp.full_like(m_ref, NEG_INF)
    l_ref[...] = jnp.zeros_like(l_ref)
    acc_ref[...] = jnp.zeros_like(acc_ref)
    heads = [slice(g * HEAD_DIM, (g + 1) * HEAD_DIM) for g in range(hps)]

    def scores(g, i):
        k0 = pl.multiple_of(i * tq, tq)
        bias = b_ref[g, pl.ds(k0, tq), :]
        return (_dot_nt(k_ref[pl.ds(k0, tq), heads[g]], q_ref[:, heads[g]])
                + jnp.concatenate([bias] * (tq // LANE), axis=1))

    def update(g, s, i):
        k0 = pl.multiple_of(i * tq, tq)
        m_prev = m_ref[g]
        m_new = jnp.maximum(m_prev, jnp.max(s, axis=0, keepdims=True))
        alpha = jnp.exp2(m_prev - m_new)
        p = jnp.exp2(s - m_new)
        l_ref[g] = alpha * l_ref[g] + jnp.sum(p, axis=0, keepdims=True)
        acc_ref[g] = alpha * acc_ref[g] + jnp.dot(vt_ref[heads[g], pl.ds(k0, tq)], p.astype(BF16),
                                                 preferred_element_type=F32)
        m_ref[g] = m_new

    for g in range(hps):
        s_ref[g, 0] = scores(g, 0)

    def body(i, carry):
        nxt = [scores(g, i + 1) for g in range(hps)]
        for g in range(hps):
            update(g, s_ref[g, i % 2], i)
        for g in range(hps):
            s_ref[g, (i + 1) % 2] = nxt[g]
        return carry

    lax.fori_loop(0, qi, body, 0)
    for g in range(hps):
        s = s_ref[g, qi % 2]
        key = lax.broadcasted_iota(jnp.int32, s.shape, 0)
        qry = lax.broadcasted_iota(jnp.int32, s.shape, 1)
        update(g, jnp.where(key <= qry, s, NEG_INF), qi)
        o_ref[:, heads[g]] = (acc_ref[g] / l_ref[g]).T.astype(o_ref.dtype)


def _flash(q, k, vt, bias, *, tq=512, hps=1):
    T, D = q.shape
    H = D // HEAD_DIM
    tq = min(tq, T)
    assert T % tq == 0 and H % hps == 0
    W = hps * HEAD_DIM
    return pl.pallas_call(
        functools.partial(_flash_kernel, tq=tq, hps=hps),
        grid=(H // hps, T // tq),
        in_specs=[
            pl.BlockSpec((tq, W), lambda h, i: (i, h)),
            pl.BlockSpec((T, W), lambda h, i: (0, h)),
            pl.BlockSpec((W, T), lambda h, i: (h, 0)),
            pl.BlockSpec((hps, T, LANE), lambda h, i: (h, 0, 0)),
        ],
        out_specs=pl.BlockSpec((tq, W), lambda h, i: (i, h)),
        out_shape=jax.ShapeDtypeStruct((T, D), BF16),
        scratch_shapes=[pltpu.VMEM((hps, 2, tq, tq), F32), pltpu.VMEM((hps, 1, tq), F32),
                        pltpu.VMEM((hps, 1, tq), F32), pltpu.VMEM((hps, HEAD_DIM, tq), F32)],
        compiler_params=_cparams(("parallel", "arbitrary")),
        name="flash",
    )(q, k, vt, bias)


def _decode_kernel(pt_ref, q_ref, *refs, pps, n_steps, n_heads):
    kc = refs[0:pps]
    vc = refs[pps:2 * pps]
    lc = refs[2 * pps:3 * pps]
    kn_ref, vn_ref, ln_ref, o_ref, m_ref, l_ref, acc_ref, carry_ref, hm_ref = refs[3 * pps:]
    p = pl.program_id(1)
    width = hm_ref.shape[1]

    @pl.when(p == 0)
    def _():
        m_ref[...] = jnp.full_like(m_ref, NEG_INF)
        l_ref[...] = jnp.zeros_like(l_ref)
        acc_ref[...] = jnp.zeros_like(acc_ref)
        carry_ref[...] = jnp.zeros_like(carry_ref)
        row = lax.broadcasted_iota(jnp.int32, hm_ref.shape, 0)
        col = lax.broadcasted_iota(jnp.int32, hm_ref.shape, 1)
        hm_ref[...] = jnp.where((col % n_heads) == (row % n_heads), 0.0, NEG_INF)

    def page_bias(lf_ref):
        x = jnp.broadcast_to(lf_ref[...], carry_ref.shape)
        lane = lax.broadcasted_iota(jnp.int32, x.shape, 1)
        tot = x
        sh = n_heads
        while sh < width:
            x = x + jnp.where(lane >= sh, pltpu.roll(x, sh, axis=1), 0.0)
            tot = tot + pltpu.roll(tot, sh, axis=1)
            sh *= 2
        c = x + carry_ref[...]
        carry_ref[...] = carry_ref[...] + tot
        return c[0:1, :] * (-LOG2E)

    def attend(ks, vs, biases, causal):
        q = q_ref[...]
        ss = []
        for kb, b in zip(ks, biases):
            s = _dot_nt(q, kb) + (hm_ref[...] + b)
            if causal:
                row = lax.broadcasted_iota(jnp.int32, s.shape, 0)
                col = lax.broadcasted_iota(jnp.int32, s.shape, 1)
                s = jnp.where(col // n_heads <= row // n_heads, s, NEG_INF)
            ss.append(s)
        m_prev = m_ref[...]
        m_new = m_prev
        for s in ss:
            m_new = jnp.maximum(m_new, jnp.max(s, axis=-1, keepdims=True))
        alpha = jnp.exp2(m_prev - m_new)
        l = alpha * l_ref[...]
        acc = alpha * acc_ref[...]
        for s, vb in zip(ss, vs):
            pr = jnp.exp2(s - m_new)
            l = l + jnp.sum(pr, axis=-1, keepdims=True)
            acc = acc + jnp.dot(pr.astype(BF16), vb, preferred_element_type=F32)
        m_ref[...] = m_new
        l_ref[...] = l
        acc_ref[...] = acc

    @pl.when(p < n_steps)
    def _():
        biases = [page_bias(r) for r in lc]
        attend([r[...].astype(BF16) for r in kc], [r[...].astype(BF16) for r in vc], biases, False)

    @pl.when(p == n_steps)
    def _():
        attend([kn_ref[...]], [vn_ref[...]], [page_bias(ln_ref)], True)
        o_ref[...] = acc_ref[...] / l_ref[...]


def _decode(page_ids, q2, cache_k, cache_v, cache_lf, k_new, v_new, lf_new, *, pps=8):
    Bd, P = page_ids.shape
    R = q2.shape[1]
    W = cache_k.shape[1]
    H = W // PAGE_SIZE
    assert P % pps == 0
    n_steps = P // pps
    last = n_steps - 1

    def page_map(i):
        return lambda b, p, pt: (pt[b, jnp.minimum(p, last) * pps + i], 0, 0)

    seq_map = lambda b, p, pt: (b, 0, 0)
    in_specs = [pl.BlockSpec((None, R, HEAD_DIM), seq_map)]
    in_specs += [pl.BlockSpec((None, W, HEAD_DIM), page_map(i)) for i in range(pps)]
    in_specs += [pl.BlockSpec((None, W, HEAD_DIM), page_map(i)) for i in range(pps)]
    in_specs += [pl.BlockSpec((None, 1, W), page_map(i)) for i in range(pps)]
    in_specs += [pl.BlockSpec((None, W, HEAD_DIM), seq_map), pl.BlockSpec((None, W, HEAD_DIM), seq_map),
                 pl.BlockSpec((None, 1, W), seq_map)]
    grid_spec = pltpu.PrefetchScalarGridSpec(
        num_scalar_prefetch=1,
        grid=(Bd, n_steps + 1),
        in_specs=in_specs,
        out_specs=pl.BlockSpec((None, R, HEAD_DIM), seq_map),
        scratch_shapes=[pltpu.VMEM((R, 1), F32), pltpu.VMEM((R, 1), F32), pltpu.VMEM((R, HEAD_DIM), F32),
                        pltpu.VMEM((SUBLANE, W), F32), pltpu.VMEM((R, W), F32)],
    )
    return pl.pallas_call(
        functools.partial(_decode_kernel, pps=pps, n_steps=n_steps, n_heads=H),
        grid_spec=grid_spec,
        out_shape=jax.ShapeDtypeStruct((Bd, R, HEAD_DIM), F32),
        compiler_params=_cparams(("parallel", "arbitrary")),
        name="decode",
    )(page_ids, q2, *([cache_k] * pps), *([cache_v] * pps), *([cache_lf] * pps), k_new, v_new, lf_new)


def _cmpx(v, i, j):
    hi = jnp.maximum(v[i], v[j])
    lo = jnp.minimum(v[i], v[j])
    v[i], v[j] = hi, lo


def _bitonic_merge_desc(v):
    n = len(v)
    j = n // 2
    while j >= 1:
        for i in range(n):
            if i & j == 0:
                _cmpx(v, i, i | j)
        j //= 2
    return v


def _sort_desc(v):
    v = list(v)
    n = len(v)
    k = 2
    while k <= n:
        j = k // 2
        while j >= 1:
            for i in range(n):
                l = i ^ j
                if l > i:
                    if (i & k) == 0:
                        _cmpx(v, i, l)
                    else:
                        _cmpx(v, l, i)
            j //= 2
        k *= 2
    return v


def _top16_sorted(v):
    groups = [_sort_desc(v[g:g + TOPK]) for g in range(0, len(v), TOPK)]
    while len(groups) > 1:
        nxt = []
        for g in range(0, len(groups), 2):
            a, b = groups[g], groups[g + 1]
            c = [jnp.maximum(a[i], b[TOPK - 1 - i]) for i in range(TOPK)]
            nxt.append(_bitonic_merge_desc(c))
        groups = nxt
    return groups[0]


def _peer_route_kernel(q_ref, kb_ref, thr_ref, e1_ref, s2_ref, e2_ref, *, n_heads):
    nrow = n_heads * N_KEYS
    tb = q_ref.shape[0]
    tops = []
    per_head = []
    for p in range(2):
        qp = q_ref[:, p * nrow:(p + 1) * nrow]
        res = _dot_nt(kb_ref[p], qp)
        srt = res[0:nrow].reshape(N_KEYS, n_heads, tb)
        per_head.append(res[nrow:2 * nrow].reshape(n_heads, N_KEYS, tb))
        tops.append(_top16_sorted([srt[k] for k in range(N_KEYS)]))
    v1, v2 = tops
    pairs = [(a, b) for a in range(TOPK) for b in range(TOPK) if (a + 1) * (b + 1) <= TOPK]
    cand = {ab: v1[ab[0]] + v2[ab[1]] for ab in pairs}
    pad = jnp.full_like(v1[0], -BIG)
    top = _top16_sorted([cand[ab] for ab in pairs] + [pad] * (4 * TOPK - len(pairs)))
    tau = top[TOPK - 1]
    z = jnp.ones_like(tau)
    for k in range(1, TOPK):
        z = z + jnp.exp(top[k] - top[0])
    inv_z = 1.0 / z
    thr_rank = []
    for a in range(TOPK):
        t = jnp.full_like(tau, BIG)
        for b in range(TOPK):
            if (a, b) in cand:
                t = jnp.minimum(t, jnp.where(cand[(a, b)] >= tau, v2[b], BIG))
        thr_rank.append(t)
    s1, s2 = per_head
    s2_ref[...] = s2
    for h in range(n_heads):
        row = lambda x: x[h:h + 1, :]
        s1h = s1[h]
        thr = jnp.full_like(s1h, BIG)
        for a in range(TOPK):
            thr = jnp.minimum(thr, jnp.where(s1h >= row(v1[a]), row(thr_rank[a]), BIG))
        thr_ref[h] = thr
        e1_ref[h] = jnp.exp(s1h - row(v1[0]))
        e2_ref[h] = jnp.exp(s2[h] - row(v2[0])) * row(inv_z)


def _peer_route(q, kb, *, n_heads, tb=256):
    N = q.shape[0]
    tb = min(tb, N)
    assert N % tb == 0
    big = lambda: pl.BlockSpec((n_heads, N_KEYS, tb), lambda i: (0, 0, i))
    shp = jax.ShapeDtypeStruct((n_heads, N_KEYS, N), F32)
    return pl.pallas_call(
        functools.partial(_peer_route_kernel, n_heads=n_heads),
        grid=(N // tb,),
        in_specs=[pl.BlockSpec((tb, q.shape[1]), lambda i: (i, 0)),
                  pl.BlockSpec(kb.shape, lambda i: (0, 0, 0))],
        out_specs=[big(), big(), big(), big()],
        out_shape=[shp, shp, shp, shp],
        compiler_params=_cparams(("parallel",)),
        name="peer_route",
    )(q, kb)


def _peer_dense_kernel(xn_ref, u_ref, v_ref, thr_ref, e1_ref, s2_ref, e2_ref, res_ref,
                       out_ref, r_ref, *, n_heads, rows_per_step):
    e = pl.program_id(1)
    slab = 4 * SUBLANE

    @pl.when(e == 0)
    def _():
        out_ref[...] = res_ref[...]

    def routing_row(r, carry):
        i1 = e * rows_per_step + r
        thr_rows = [thr_ref[h, pl.ds(i1, 1), :] for h in range(n_heads)]
        e1_rows = [e1_ref[h, pl.ds(i1, 1), :] for h in range(n_heads)]
        for b in range(N_KEYS // slab):
            rs = slice(b * slab, (b + 1) * slab)
            acc = None
            for h in range(n_heads):
                sel = jnp.where(s2_ref[h, rs, :] >= thr_rows[h], e2_ref[h, rs, :], 0.0)
                term = sel * e1_rows[h]
                acc = term if acc is None else acc + term
            r_ref[pl.ds(pl.multiple_of(r * N_KEYS + b * slab, slab), slab), :] = acc
        return carry

    lax.fori_loop(0, rows_per_step, routing_row, 0)
    act = _dot_nt(u_ref[...], xn_ref[...])
    wt = (_gelu(act) * r_ref[...]).astype(BF16)
    out_ref[...] += _dot_tn(wt, v_ref[...])


def _peer_dense(xn, u, v, thr, e1, s2, e2, resid, *, tb=512, te=1024):
    N, D = xn.shape
    E = u.shape[0]
    n_heads = thr.shape[0]
    tb = min(tb, N)
    assert N % tb == 0 and E % te == 0 and te % N_KEYS == 0
    rps = te // N_KEYS
    once = pl.Buffered(1)
    tok = lambda: pl.BlockSpec((n_heads, N_KEYS, tb), lambda i, e: (0, 0, i), pipeline_mode=once)
    return pl.pallas_call(
        functools.partial(_peer_dense_kernel, n_heads=n_heads, rows_per_step=rps),
        grid=(N // tb, E // te),
        in_specs=[
            pl.BlockSpec((tb, D), lambda i, e: (i, 0), pipeline_mode=once),
            pl.BlockSpec((te, D), lambda i, e: (e, 0)),
            pl.BlockSpec((te, D), lambda i, e: (e, 0)),
            tok(), tok(), tok(), tok(),
            pl.BlockSpec((tb, D), lambda i, e: (i, 0), pipeline_mode=once),
        ],
        out_specs=pl.BlockSpec((tb, D), lambda i, e: (i, 0)),
        out_shape=jax.ShapeDtypeStruct((N, D), F32),
        scratch_shapes=[pltpu.VMEM((te, tb), F32)],
        compiler_params=_cparams(("parallel", "arbitrary")),
        name="peer_dense",
    )(xn, u, v, thr, e1, s2, e2, resid)


def _peer_weights(w_q, sub_keys):
    D = w_q.shape[0]
    H = sub_keys.shape[0]
    half = sub_keys.shape[-1]
    wq = w_q.reshape(D, H, 2, half).transpose(0, 2, 1, 3).reshape(D, 2 * H * half).astype(BF16)
    eye = jnp.eye(H, dtype=sub_keys.dtype)
    sk = sub_keys.transpose(1, 0, 2, 3)
    bd = sk[:, :, :, None, :] * eye[None, :, None, :, None]
    hk = bd.reshape(2, H * N_KEYS, H * half)
    kh = bd.transpose(0, 2, 1, 3, 4).reshape(2, N_KEYS * H, H * half)
    return wq, jnp.concatenate([kh, hk], axis=1).astype(BF16)


def _peer(x, gain, wq, kb, u, v, *, n_heads):
    xn = _rms_norm(x, gain)
    q = _mm(xn, wq, out_dtype=BF16)
    thr, e1, s2, e2 = _peer_route(q, kb, n_heads=n_heads)
    return _peer_dense(xn, u, v, thr, e1, s2, e2, x)


def _pad_lanes(a, width=LANE):
    return jnp.pad(a, [(0, 0)] * (a.ndim - 1) + [(0, width - a.shape[-1])])


def kernel(x_prompt, x_sample, state_conv, state_h, cache_k, cache_v, cache_logf, page_table,
           norm_mix, norm_ffn,
           rec_w_in, rec_conv_w, rec_conv_b, rec_w_a, rec_b_a, rec_w_i, rec_b_i, rec_lambda, rec_w_out,
           att_w_in, att_b_f, att_q_norm, att_k_norm, att_w_out,
           peer_w_q, peer_sub_keys, peer_u, peer_v):
    Bp, S, D = x_prompt.shape
    Bd, Sd, _ = x_sample.shape
    assert Bp == 1
    H = D // HEAD_DIM
    PH = peer_sub_keys.shape[1]
    depth = norm_mix.shape[0]
    n_att, n_pool = cache_k.shape[0], cache_k.shape[1]
    page_w = PAGE_SIZE * H

    xp = x_prompt.reshape(S, D)
    xs = x_sample.reshape(Bd * Sd, D)
    row = lambda a: a.reshape(1, -1).astype(F32)
    cache_k2 = cache_k.reshape(n_att * n_pool, page_w, HEAD_DIM)
    cache_v2 = cache_v.reshape(n_att * n_pool, page_w, HEAD_DIM)
    cache_lf2 = cache_logf.reshape(n_att * n_pool, 1, page_w)

    pc, ph, pk, pv, plf = [], [], [], [], []
    sc, sh, sk, sv, slf = [], [], [], [], []
    for i in range(depth):
        j = i // 2
        if i % 2 == 0:
            w_in = _to_bf16(rec_w_in, j)
            w_out = _to_bf16(rec_w_out, j)
            wa = rec_w_a[j].astype(BF16)
            wi = rec_w_i[j].astype(BF16)
            cw, cb = rec_conv_w[j], row(rec_conv_b[j])
            ba, bi, lam = row(rec_b_a[j]), row(rec_b_i[j]), row(rec_lambda[j])
            proj_p = _mm(xp, w_in, gain=norm_mix[i])
            yp, hl_p = _rglru_prompt(proj_p, jnp.zeros((SUBLANE, D), F32), jnp.zeros((1, D), F32),
                                     cw, cb, wa, wi, ba, bi, lam)
            pc.append(proj_p[S - (CONV_W - 1):, D:].reshape(1, CONV_W - 1, D))
            ph.append(hl_p)
            xp = _mm(yp, w_out, residual=xp)
            proj_s = _mm(xs, w_in, gain=norm_mix[i]).reshape(Bd, Sd, 2 * D)
            ext = jnp.concatenate([state_conv[j], proj_s[..., D:]], axis=1)
            ys_tm, hl_s = _rglru_sample(proj_s[..., :D].transpose(1, 0, 2), ext.transpose(1, 0, 2),
                                        state_h[j], cw, cb, wa, wi, ba, bi, lam)
            sc.append(ext[:, Sd:])
            sh.append(hl_s)
            xs = _mm(ys_tm.transpose(1, 0, 2).reshape(Bd * Sd, D), w_out, residual=xs)
        else:
            w_qkv = att_w_in[j][:, :3 * D].astype(BF16)
            w_f = _pad_lanes(att_w_in[j][:, 3 * D:]).astype(BF16)
            w_out = _to_bf16(att_w_out, j)
            b_f = _pad_lanes(row(att_b_f[j]))
            qg, kg = row(att_q_norm[j]), row(att_k_norm[j])
            proj = _mm(xp, w_qkv, gain=norm_mix[i])
            projf = _mm(xp, w_f, gain=norm_mix[i])
            qb, kf, kbf, vt, lf, bias = _qkv_post(proj, projf, b_f, qg, kg, prompt=True)
            op = _flash(qb, kbf, vt, bias)
            pk.append(kf.reshape(1, S, H, HEAD_DIM))
            pv.append(proj[:, 2 * D:].reshape(1, S, H, HEAD_DIM))
            plf.append(lf[:, :H].reshape(1, S, H))
            xp = _mm(op, w_out, residual=xp)
            n_s = Bd * Sd
            proj = _mm(xs, w_qkv, gain=norm_mix[i])
            projf = _mm(xs, w_f, gain=norm_mix[i])
            qb, kf, kbf, vbf, lf = _qkv_post(proj, projf, b_f, qg, kg, prompt=False)
            lf_s = lf[:, :H].reshape(Bd, Sd, H)
            as_page = lambda a: jnp.pad(a.reshape(Bd, Sd * H, -1),
                                        ((0, 0), (0, page_w - Sd * H), (0, 0)))
            o = _decode(page_table + j * n_pool, qb.reshape(Bd, Sd * H, HEAD_DIM),
                        cache_k2, cache_v2, cache_lf2,
                        as_page(kbf), as_page(vbf), as_page(lf_s).reshape(Bd, 1, page_w))
            sk.append(kf.reshape(Bd, Sd, H, HEAD_DIM))
            sv.append(proj[:, 2 * D:].reshape(Bd, Sd, H, HEAD_DIM))
            slf.append(lf_s)
            xs = _mm(o.reshape(n_s, D), w_out, residual=xs)
        wq, kb = _peer_weights(peer_w_q[i], peer_sub_keys[i])
        u = _to_bf16(peer_u, i)
        v = _to_bf16(peer_v, i)
        xp = _peer(xp, norm_ffn[i], wq, kb, u, v, n_heads=PH)
        xs = _peer(xs, norm_ffn[i], wq, kb, u, v, n_heads=PH)

    return (xp.reshape(1, S, D), xs.reshape(Bd, Sd, D),
            jnp.stack(pc), jnp.stack(ph), jnp.stack(pk), jnp.stack(pv), jnp.stack(plf),
            jnp.stack(sc), jnp.stack(sh), jnp.stack(sk), jnp.stack(sv), jnp.stack(slf))
```

```python
import functools
import math

import jax
import jax.numpy as jnp
from jax import lax
from jax.experimental import pallas as pl
from jax.experimental.pallas import tpu as pltpu

F32 = jnp.float32
BF16 = jnp.bfloat16

EPS = 1e-6
NEG_INF = -1e30
BIG = 3.0e38
LOG2E = math.log2(math.e)
RG_C = 8.0
CONV_W = 4
HEAD_DIM = 128
PAGE_SIZE = 128
TOPK = 16
N_KEYS = 128
LANE = 128
SUBLANE = 8
VMEM_LIMIT = 56 * 1024 * 1024


def _cparams(sem, flags=None):
    return pltpu.CompilerParams(dimension_semantics=sem, vmem_limit_bytes=VMEM_LIMIT, flags=flags)


def _largest_tile(n, target, unit):
    t = min(target, n) // unit * unit
    while n % t:
        t -= unit
    return t


def _gelu(x):
    c = math.sqrt(2.0 / math.pi)
    return 0.5 * x * (1.0 + jnp.tanh(x * (c + (0.044715 * c) * (x * x))))


def _log_sigmoid(z):
    return jnp.minimum(z, 0.0) - jnp.log1p(jnp.exp(-jnp.abs(z)))


def _sigmoid(z):
    return 1.0 / (1.0 + jnp.exp(-z))


def _dot_nt(a, b):
    return lax.dot_general(a, b, (((1,), (1,)), ((), ())), preferred_element_type=F32)


def _dot_tn(a, b):
    return lax.dot_general(a, b, (((0,), (0,)), ((), ())), preferred_element_type=F32)


def _mm_kernel(*refs, norm, residual):
    it = iter(refs)
    x_ref = next(it)
    g_ref = next(it) if norm else None
    w_ref = next(it)
    res_ref = next(it) if residual else None
    out_ref = next(it)
    xn_ref = next(it) if norm else None

    if norm:
        @pl.when(pl.program_id(1) == 0)
        def _():
            xf = x_ref[...].astype(F32)
            y = xf * lax.rsqrt(jnp.mean(xf * xf, axis=-1, keepdims=True) + EPS)
            xn_ref[...] = (y * g_ref[...]).astype(BF16)
        lhs = xn_ref[...]
    else:
        lhs = x_ref[...].astype(BF16)
    acc = jnp.dot(lhs, w_ref[...], preferred_element_type=F32)
    if residual:
        acc = acc + res_ref[...]
    out_ref[...] = acc.astype(out_ref.dtype)


def _mm(x, w, *, gain=None, residual=None, out_dtype=F32, tm=1024, tn=1024):
    M, K = x.shape
    N = w.shape[1]
    tm = _largest_tile(M, tm, SUBLANE)
    tn = _largest_tile(N, tn, LANE)
    norm = gain is not None
    in_specs = [pl.BlockSpec((tm, K), lambda i, j: (i, 0))]
    args = [x]
    if norm:
        in_specs.append(pl.BlockSpec((1, K), lambda i, j: (0, 0)))
        args.append(gain.reshape(1, K).astype(F32))
    in_specs.append(pl.BlockSpec((K, tn), lambda i, j: (0, j)))
    args.append(w)
    if residual is not None:
        in_specs.append(pl.BlockSpec((tm, tn), lambda i, j: (i, j)))
        args.append(residual)
    scratch = [pltpu.VMEM((tm, K), BF16)] if norm else []
    return pl.pallas_call(
        functools.partial(_mm_kernel, norm=norm, residual=residual is not None),
        grid=(M // tm, N // tn),
        in_specs=in_specs,
        out_specs=pl.BlockSpec((tm, tn), lambda i, j: (i, j)),
        out_shape=jax.ShapeDtypeStruct((M, N), out_dtype),
        scratch_shapes=scratch,
        compiler_params=_cparams(("parallel", "arbitrary")),
        name="mm",
    )(*args)


def _cast_kernel(x_ref, o_ref):
    o_ref[...] = x_ref[...].astype(o_ref.dtype)


def _to_bf16(w, layer, *, rows=1024):
    _, M, K = w.shape
    rows = min(rows, M)
    assert M % rows == 0
    return pl.pallas_call(
        _cast_kernel,
        grid=(M // rows,),
        in_specs=[pl.BlockSpec((None, rows, K), lambda i: (layer, i, 0))],
        out_specs=pl.BlockSpec((rows, K), lambda i: (i, 0)),
        out_shape=jax.ShapeDtypeStruct((M, K), BF16),
        compiler_params=_cparams(("parallel",)),
        name="to_bf16",
    )(w)


def _norm_kernel(x_ref, g_ref, o_ref):
    xf = x_ref[...].astype(F32)
    y = xf * lax.rsqrt(jnp.mean(xf * xf, axis=-1, keepdims=True) + EPS)
    o_ref[...] = (y * g_ref[...]).astype(o_ref.dtype)


def _rms_norm(x, gain, *, tm=512):
    M, K = x.shape
    tm = min(tm, M)
    assert M % tm == 0
    return pl.pallas_call(
        _norm_kernel,
        grid=(M // tm,),
        in_specs=[pl.BlockSpec((tm, K), lambda i: (i, 0)), pl.BlockSpec((1, K), lambda i: (0, 0))],
        out_specs=pl.BlockSpec((tm, K), lambda i: (i, 0)),
        out_shape=jax.ShapeDtypeStruct((M, K), BF16),
        compiler_params=_cparams(("parallel",)),
        name="rms_norm",
    )(x, gain.reshape(1, K).astype(F32))


def _rg_coeffs(xc, w_a, w_i, b_a, b_i, lam):
    xcb = xc.astype(BF16)
    r = _sigmoid(jnp.dot(xcb, w_a, preferred_element_type=F32) + b_a)
    i = _sigmoid(jnp.dot(xcb, w_i, preferred_element_type=F32) + b_i)
    log_a = RG_C * r * _log_sigmoid(lam)
    a = jnp.exp(log_a)
    th = jnp.tanh(log_a)
    b = jnp.sqrt(-2.0 * th / (1.0 - th)) * (i * xc)
    return a, b


def _scan_rows(a, b):
    n = a.shape[0]
    row = lax.broadcasted_iota(jnp.int32, a.shape, 0)
    s = 1
    while s < n:
        a_sh = pltpu.roll(a, s, axis=0)
        b_sh = pltpu.roll(b, s, axis=0)
        valid = row >= s
        b = jnp.where(valid, a * b_sh + b, b)
        a = jnp.where(valid, a * a_sh, a)
        s *= 2
    return a, b


def _cumsum_rows(x):
    n = x.shape[0]
    row = lax.broadcasted_iota(jnp.int32, x.shape, 0)
    s = 1
    while s < n:
        x = x + jnp.where(row >= s, pltpu.roll(x, s, axis=0), 0.0)
        s *= 2
    return x


def _rglru_prompt_kernel(gate_ref, xr_ref, cw_ref, cb_ref, wa_ref, wi_ref, ba_ref, bi_ref,
                         lam_ref, init_ref, h0_ref, y_ref, hl_ref, ext_ref, h_ref, *, tt):
    t = pl.program_id(1)

    @pl.when(t == 0)
    def _():
        ext_ref[0:SUBLANE, :] = init_ref[...]
        h_ref[...] = h0_ref[...]

    xr = xr_ref[...]
    ext_ref[SUBLANE:SUBLANE + tt, :] = xr
    xc = cb_ref[...] + xr * cw_ref[CONV_W - 1:CONV_W, :]
    for k in range(CONV_W - 1):
        xc = xc + ext_ref[pl.ds(SUBLANE - (CONV_W - 1) + k, tt), :] * cw_ref[k:k + 1, :]
    ext_ref[0:SUBLANE, :] = xr[tt - SUBLANE:tt, :]

    a, b = _rg_coeffs(xc, wa_ref[...], wi_ref[...], ba_ref[...], bi_ref[...], lam_ref[...])
    A, B = _scan_rows(a, b)
    h = A * h_ref[...] + B
    h_ref[...] = h[tt - 1:tt, :]
    hl_ref[...] = h[tt - 1:tt, :]
    y_ref[...] = (h * _gelu(gate_ref[...])).astype(y_ref.dtype)


def _rglru_prompt(proj, conv_init, h0, cw, cb, wa, wi, ba, bi, lam, *, tt=256):
    T = proj.shape[0]
    D = proj.shape[1] // 2
    nb, C = wa.shape[0], wa.shape[1]
    tt = min(tt, T)
    assert T % tt == 0 and tt % SUBLANE == 0
    vec = lambda: pl.BlockSpec((1, C), lambda n, t: (0, n))
    y, hl = pl.pallas_call(
        functools.partial(_rglru_prompt_kernel, tt=tt),
        grid=(nb, T // tt),
        in_specs=[
            pl.BlockSpec((tt, C), lambda n, t: (t, n)),
            pl.BlockSpec((tt, C), lambda n, t: (t, nb + n)),
            pl.BlockSpec((CONV_W, C), lambda n, t: (0, n)),
            vec(),
            pl.BlockSpec((None, C, C), lambda n, t: (n, 0, 0)),
            pl.BlockSpec((None, C, C), lambda n, t: (n, 0, 0)),
            vec(), vec(), vec(),
            pl.BlockSpec((SUBLANE, C), lambda n, t: (0, n)),
            vec(),
        ],
        out_specs=[pl.BlockSpec((tt, C), lambda n, t: (t, n)),
                   pl.BlockSpec((1, C), lambda n, t: (0, n))],
        out_shape=[jax.ShapeDtypeStruct((T, D), BF16), jax.ShapeDtypeStruct((1, D), F32)],
        scratch_shapes=[pltpu.VMEM((tt + SUBLANE, C), F32), pltpu.VMEM((1, C), F32)],
        compiler_params=_cparams(("parallel", "arbitrary")),
        name="rglru_prompt",
    )(proj, proj, cw, cb, wa, wi, ba, bi, lam, conv_init, h0)
    return y, hl


def _rglru_sample_kernel(gate_ref, ext_ref, cw_ref, cb_ref, wa_ref, wi_ref, ba_ref, bi_ref,
                         lam_ref, h0_ref, y_ref, hl_ref, *, steps):
    h = h0_ref[...]
    for t in range(steps):
        xc = cb_ref[...]
        for k in range(CONV_W):
            xc = xc + ext_ref[t + k] * cw_ref[k:k + 1, :]
        a, b = _rg_coeffs(xc, wa_ref[...], wi_ref[...], ba_ref[...], bi_ref[...], lam_ref[...])
        h = a * h + b
        y_ref[t] = (h * _gelu(gate_ref[t])).astype(y_ref.dtype)
    hl_ref[...] = h


def _rglru_sample(gate_tm, ext_tm, h0, cw, cb, wa, wi, ba, bi, lam):
    S, B, D = gate_tm.shape
    nb, C = wa.shape[0], wa.shape[1]
    vec = lambda: pl.BlockSpec((1, C), lambda n: (0, n))
    return pl.pallas_call(
        functools.partial(_rglru_sample_kernel, steps=S),
        grid=(nb,),
        in_specs=[
            pl.BlockSpec((S, B, C), lambda n: (0, 0, n)),
            pl.BlockSpec((S + CONV_W - 1, B, C), lambda n: (0, 0, n)),
            pl.BlockSpec((CONV_W, C), lambda n: (0, n)),
            vec(),
            pl.BlockSpec((None, C, C), lambda n: (n, 0, 0)),
            pl.BlockSpec((None, C, C), lambda n: (n, 0, 0)),
            vec(), vec(), vec(),
            pl.BlockSpec((B, C), lambda n: (0, n)),
        ],
        out_specs=[pl.BlockSpec((S, B, C), lambda n: (0, 0, n)),
                   pl.BlockSpec((B, C), lambda n: (0, n))],
        out_shape=[jax.ShapeDtypeStruct((S, B, D), BF16), jax.ShapeDtypeStruct((B, D), F32)],
        compiler_params=_cparams(("parallel",)),
        name="rglru_sample",
    )(gate_tm, ext_tm, cw, cb, wa, wi, ba, bi, lam, h0)


def _qkv_post_kernel(q_ref, k_ref, v_ref, f_ref, bf_ref, qg_ref, kg_ref, *refs, prompt):
    if prompt:
        qb_ref, kf_ref, kb_ref, vt_ref, lf_ref, bias_ref, carry_ref = refs
    else:
        qb_ref, kf_ref, kb_ref, vb_ref, lf_ref = refs
    n_heads = q_ref.shape[1] // HEAD_DIM
    q_scale = HEAD_DIM ** -0.5 * LOG2E
    for h in range(n_heads):
        sl = slice(h * HEAD_DIM, (h + 1) * HEAD_DIM)
        qh = q_ref[:, sl]
        qn = qh * lax.rsqrt(jnp.mean(qh * qh, axis=-1, keepdims=True) + EPS) * qg_ref[...]
        qb_ref[:, sl] = (qn * q_scale).astype(BF16)
        kh = k_ref[:, sl]
        kn = kh * lax.rsqrt(jnp.mean(kh * kh, axis=-1, keepdims=True) + EPS) * kg_ref[...]
        kf_ref[:, sl] = kn
        kb_ref[:, sl] = kn.astype(BF16)
        if prompt:
            vt_ref[sl, :] = v_ref[:, sl].T.astype(BF16)
    if not prompt:
        vb_ref[...] = v_ref[...].astype(BF16)
    lf = _log_sigmoid(f_ref[...] + bf_ref[...])
    lf_ref[...] = lf
    if prompt:
        @pl.when(pl.program_id(0) == 0)
        def _():
            carry_ref[...] = jnp.zeros_like(carry_ref)

        c = _cumsum_rows(lf) + carry_ref[...]
        carry_ref[...] = c[c.shape[0] - 1:, :]
        nb = c * (-LOG2E)
        for h in range(n_heads):
            bias_ref[h] = jnp.broadcast_to(nb[:, h:h + 1], (nb.shape[0], LANE))


def _qkv_post(proj, projf, b_f, q_g, k_g, *, prompt, tm=256):
    T = proj.shape[0]
    D = proj.shape[1] // 3
    H = D // HEAD_DIM
    tm = min(tm, T)
    assert T % tm == 0
    blk = lambda j: pl.BlockSpec((tm, D), lambda i: (i, j))
    small = pl.BlockSpec((tm, LANE), lambda i: (i, 0))
    vec = pl.BlockSpec((1, LANE), lambda i: (0, 0))
    out_specs = [blk(0), blk(0), blk(0)]
    out_shape = [jax.ShapeDtypeStruct((T, D), BF16), jax.ShapeDtypeStruct((T, D), F32),
                 jax.ShapeDtypeStruct((T, D), BF16)]
    if prompt:
        out_specs += [pl.BlockSpec((D, tm), lambda i: (0, i)), small,
                      pl.BlockSpec((H, tm, LANE), lambda i: (0, i, 0))]
        out_shape += [jax.ShapeDtypeStruct((D, T), BF16), jax.ShapeDtypeStruct((T, LANE), F32),
                      jax.ShapeDtypeStruct((H, T, LANE), F32)]
        scratch = [pltpu.VMEM((1, LANE), F32)]
    else:
        out_specs += [blk(0), small]
        out_shape += [jax.ShapeDtypeStruct((T, D), BF16), jax.ShapeDtypeStruct((T, LANE), F32)]
        scratch = []
    return pl.pallas_call(
        functools.partial(_qkv_post_kernel, prompt=prompt),
        grid=(T // tm,),
        in_specs=[blk(0), blk(1), blk(2), small, vec, vec, vec],
        out_specs=out_specs,
        out_shape=out_shape,
        scratch_shapes=scratch,
        compiler_params=_cparams(("arbitrary",)),
        name="qkv_post",
    )(proj, proj, proj, projf, b_f, q_g, k_g)


def _flash_kernel(q_ref, k_ref, vt_ref, b_ref, o_ref, s_ref, m_ref, l_ref, acc_ref, *, tq, hps):
    qi = pl.program_id(1)
    m_ref[...] = jnp.full_like(m_ref, NEG_INF)
    l_ref[...] = jnp.zeros_like(l_ref)
    acc_ref[...] = jnp.zeros_like(acc_ref)
    heads = [slice(g * HEAD_DIM, (g + 1) * HEAD_DIM) for g in range(hps)]

    def scores(g, i):
        k0 = pl.multiple_of(i * tq, tq)
        bias = b_ref[g, pl.ds(k0, tq), :]
        return (_dot_nt(k_ref[pl.ds(k0, tq), heads[g]], q_ref[:, heads[g]])
                + jnp.concatenate([bias] * (tq // LANE), axis=1))

    def update(g, s, i):
        k0 = pl.multiple_of(i * tq, tq)
        m_prev = m_ref[g]
        m_new = jnp.maximum(m_prev, jnp.max(s, axis=0, keepdims=True))
        alpha = jnp.exp2(m_prev - m_new)
        p = jnp.exp2(s - m_new)
        l_ref[g] = alpha * l_ref[g] + jnp.sum(p, axis=0, keepdims=True)
        acc_ref[g] = alpha * acc_ref[g] + jnp.dot(vt_ref[heads[g], pl.ds(k0, tq)], p.astype(BF16),
                                                 preferred_element_type=F32)
        m_ref[g] = m_new

    for g in range(hps):
        s_ref[g, 0] = scores(g, 0)

    def body(i, carry):
        nxt = [scores(g, i + 1) for g in range(hps)]
        for g in range(hps):
            update(g, s_ref[g, i % 2], i)
        for g in range(hps):
            s_ref[g, (i + 1) % 2] = nxt[g]
        return carry

    lax.fori_loop(0, qi, body, 0)
    for g in range(hps):
        s = s_ref[g, qi % 2]
        key = lax.broadcasted_iota(jnp.int32, s.shape, 0)
        qry = lax.broadcasted_iota(jnp.int32, s.shape, 1)
        update(g, jnp.where(key <= qry, s, NEG_INF), qi)
        o_ref[:, heads[g]] = (acc_ref[g] / l_ref[g]).T.astype(o_ref.dtype)


def _flash(q, k, vt, bias, *, tq=512, hps=1):
    T, D = q.shape
    H = D // HEAD_DIM
    tq = min(tq, T)
    assert T % tq == 0 and H % hps == 0
    W = hps * HEAD_DIM
    return pl.pallas_call(
        functools.partial(_flash_kernel, tq=tq, hps=hps),
        grid=(H // hps, T // tq),
        in_specs=[
            pl.BlockSpec((tq, W), lambda h, i: (i, h)),
            pl.BlockSpec((T, W), lambda h, i: (0, h)),
            pl.BlockSpec((W, T), lambda h, i: (h, 0)),
            pl.BlockSpec((hps, T, LANE), lambda h, i: (h, 0, 0)),
        ],
        out_specs=pl.BlockSpec((tq, W), lambda h, i: (i, h)),
        out_shape=jax.ShapeDtypeStruct((T, D), BF16),
        scratch_shapes=[pltpu.VMEM((hps, 2, tq, tq), F32), pltpu.VMEM((hps, 1, tq), F32),
                        pltpu.VMEM((hps, 1, tq), F32), pltpu.VMEM((hps, HEAD_DIM, tq), F32)],
        compiler_params=_cparams(("parallel", "arbitrary")),
        name="flash",
    )(q, k, vt, bias)


def _decode_kernel(pt_ref, q_ref, *refs, pps, n_steps, n_heads):
    kc = refs[0:pps]
    vc = refs[pps:2 * pps]
    lc = refs[2 * pps:3 * pps]
    kn_ref, vn_ref, ln_ref, o_ref, m_ref, l_ref, acc_ref, carry_ref, hm_ref = refs[3 * pps:]
    p = pl.program_id(1)
    width = hm_ref.shape[1]

    @pl.when(p == 0)
    def _():
        m_ref[...] = jnp.full_like(m_ref, NEG_INF)
        l_ref[...] = jnp.zeros_like(l_ref)
        acc_ref[...] = jnp.zeros_like(acc_ref)
        carry_ref[...] = jnp.zeros_like(carry_ref)
        row = lax.broadcasted_iota(jnp.int32, hm_ref.shape, 0)
        col = lax.broadcasted_iota(jnp.int32, hm_ref.shape, 1)
        hm_ref[...] = jnp.where((col % n_heads) == (row % n_heads), 0.0, NEG_INF)

    def page_bias(lf_ref):
        x = jnp.broadcast_to(lf_ref[...], carry_ref.shape)
        lane = lax.broadcasted_iota(jnp.int32, x.shape, 1)
        tot = x
        sh = n_heads
        while sh < width:
            x = x + jnp.where(lane >= sh, pltpu.roll(x, sh, axis=1), 0.0)
            tot = tot + pltpu.roll(tot, sh, axis=1)
            sh *= 2
        c = x + carry_ref[...]
        carry_ref[...] = carry_ref[...] + tot
        return c[0:1, :] * (-LOG2E)

    def attend(ks, vs, biases, causal):
        q = q_ref[...]
        ss = []
        for kb, b in zip(ks, biases):
            s = _dot_nt(q, kb) + (hm_ref[...] + b)
            if causal:
                row = lax.broadcasted_iota(jnp.int32, s.shape, 0)
                col = lax.broadcasted_iota(jnp.int32, s.shape, 1)
                s = jnp.where(col // n_heads <= row // n_heads, s, NEG_INF)
            ss.append(s)
        m_prev = m_ref[...]
        m_new = m_prev
        for s in ss:
            m_new = jnp.maximum(m_new, jnp.max(s, axis=-1, keepdims=True))
        alpha = jnp.exp2(m_prev - m_new)
        l = alpha * l_ref[...]
        acc = alpha * acc_ref[...]
        for s, vb in zip(ss, vs):
            pr = jnp.exp2(s - m_new)
            l = l + jnp.sum(pr, axis=-1, keepdims=True)
            acc = acc + jnp.dot(pr.astype(BF16), vb, preferred_element_type=F32)
        m_ref[...] = m_new
        l_ref[...] = l
        acc_ref[...] = acc

    @pl.when(p < n_steps)
    def _():
        biases = [page_bias(r) for r in lc]
        attend([r[...].astype(BF16) for r in kc], [r[...].astype(BF16) for r in vc], biases, False)

    @pl.when(p == n_steps)
    def _():
        attend([kn_ref[...]], [vn_ref[...]], [page_bias(ln_ref)], True)
        o_ref[...] = acc_ref[...] / l_ref[...]


def _decode(page_ids, q2, cache_k, cache_v, cache_lf, k_new, v_new, lf_new, *, pps=8):
    Bd, P = page_ids.shape
    R = q2.shape[1]
    W = cache_k.shape[1]
    H = W // PAGE_SIZE
    assert P % pps == 0
    n_steps = P // pps
    last = n_steps - 1

    def page_map(i):
        return lambda b, p, pt: (pt[b, jnp.minimum(p, last) * pps + i], 0, 0)

    seq_map = lambda b, p, pt: (b, 0, 0)
    in_specs = [pl.BlockSpec((None, R, HEAD_DIM), seq_map)]
    in_specs += [pl.BlockSpec((None, W, HEAD_DIM), page_map(i)) for i in range(pps)]
    in_specs += [pl.BlockSpec((None, W, HEAD_DIM), page_map(i)) for i in range(pps)]
    in_specs += [pl.BlockSpec((None, 1, W), page_map(i)) for i in range(pps)]
    in_specs += [pl.BlockSpec((None, W, HEAD_DIM), seq_map), pl.BlockSpec((None, W, HEAD_DIM), seq_map),
                 pl.BlockSpec((None, 1, W), seq_map)]
    grid_spec = pltpu.PrefetchScalarGridSpec(
        num_scalar_prefetch=1,
        grid=(Bd, n_steps + 1),
        in_specs=in_specs,
        out_specs=pl.BlockSpec((None, R, HEAD_DIM), seq_map),
        scratch_shapes=[pltpu.VMEM((R, 1), F32), pltpu.VMEM((R, 1), F32), pltpu.VMEM((R, HEAD_DIM), F32),
                        pltpu.VMEM((SUBLANE, W), F32), pltpu.VMEM((R, W), F32)],
    )
    return pl.pallas_call(
        functools.partial(_decode_kernel, pps=pps, n_steps=n_steps, n_heads=H),
        grid_spec=grid_spec,
        out_shape=jax.ShapeDtypeStruct((Bd, R, HEAD_DIM), F32),
        compiler_params=_cparams(("parallel", "arbitrary")),
        name="decode",
    )(page_ids, q2, *([cache_k] * pps), *([cache_v] * pps), *([cache_lf] * pps), k_new, v_new, lf_new)


def _cmpx(v, i, j):
    hi = jnp.maximum(v[i], v[j])
    lo = jnp.minimum(v[i], v[j])
    v[i], v[j] = hi, lo


def _bitonic_merge_desc(v):
    n = len(v)
    j = n // 2
    while j >= 1:
        for i in range(n):
            if i & j == 0:
                _cmpx(v, i, i | j)
        j //= 2
    return v


def _sort_desc(v):
    v = list(v)
    n = len(v)
    k = 2
    while k <= n:
        j = k // 2
        while j >= 1:
            for i in range(n):
                l = i ^ j
                if l > i:
                    if (i & k) == 0:
                        _cmpx(v, i, l)
                    else:
                        _cmpx(v, l, i)
            j //= 2
        k *= 2
    return v


def _top16_sorted(v):
    groups = [_sort_desc(v[g:g + TOPK]) for g in range(0, len(v), TOPK)]
    while len(groups) > 1:
        nxt = []
        for g in range(0, len(groups), 2):
            a, b = groups[g], groups[g + 1]
            c = [jnp.maximum(a[i], b[TOPK - 1 - i]) for i in range(TOPK)]
            nxt.append(_bitonic_merge_desc(c))
        groups = nxt
    return groups[0]


def _peer_route_kernel(q_ref, kb_ref, cnt_ref, e1_ref, r2_ref, e2_ref, *, n_heads):
    nrow = n_heads * N_KEYS
    tb = q_ref.shape[0]
    tops = []
    per_head = []
    for p in range(2):
        qp = q_ref[:, p * nrow:(p + 1) * nrow]
        res = _dot_nt(kb_ref[p], qp)
        srt = res[0:nrow].reshape(N_KEYS, n_heads, tb)
        per_head.append(res[nrow:2 * nrow].reshape(n_heads, N_KEYS, tb))
        tops.append(_top16_sorted([srt[k] for k in range(N_KEYS)]))
    v1, v2 = tops
    pairs = [(a, b) for a in range(TOPK) for b in range(TOPK) if (a + 1) * (b + 1) <= TOPK]
    cand = {ab: v1[ab[0]] + v2[ab[1]] for ab in pairs}
    pad = jnp.full_like(v1[0], -BIG)
    top = _top16_sorted([cand[ab] for ab in pairs] + [pad] * (4 * TOPK - len(pairs)))
    tau = top[TOPK - 1]
    z = jnp.ones_like(tau)
    for k in range(1, TOPK):
        z = z + jnp.exp(top[k] - top[0])
    inv_z = 1.0 / z
    cnt_rank = []
    for a in range(TOPK):
        n = jnp.zeros_like(tau)
        for b in range(TOPK):
            if (a, b) in cand:
                n = n + jnp.where(cand[(a, b)] >= tau, 1.0, 0.0)
        cnt_rank.append(n)
    s1, s2 = per_head
    for h in range(n_heads):
        row = lambda x: x[h:h + 1, :]
        s1h, s2h = s1[h], s2[h]
        cnt = jnp.zeros_like(s1h)
        r2 = jnp.zeros_like(s2h)
        for a in range(TOPK):
            cnt = jnp.maximum(cnt, jnp.where(s1h >= row(v1[a]), row(cnt_rank[a]), 0.0))
            r2 = r2 + jnp.where(row(v2[a]) > s2h, 1.0, 0.0)
        cnt_ref[h] = cnt
        r2_ref[h] = r2.astype(BF16)
        e1_ref[h] = jnp.exp(s1h - row(v1[0]))
        e2_ref[h] = (jnp.exp(s2h - row(v2[0])) * row(inv_z)).astype(BF16)


def _peer_route(q, kb, *, n_heads, tb=256):
    N = q.shape[0]
    tb = min(tb, N)
    assert N % tb == 0
    big = lambda: pl.BlockSpec((n_heads, N_KEYS, tb), lambda i: (0, 0, i))
    shp = lambda dt: jax.ShapeDtypeStruct((n_heads, N_KEYS, N), dt)
    return pl.pallas_call(
        functools.partial(_peer_route_kernel, n_heads=n_heads),
        grid=(N // tb,),
        in_specs=[pl.BlockSpec((tb, q.shape[1]), lambda i: (i, 0)),
                  pl.BlockSpec(kb.shape, lambda i: (0, 0, 0))],
        out_specs=[big(), big(), big(), big()],
        out_shape=[shp(F32), shp(F32), shp(BF16), shp(BF16)],
        compiler_params=_cparams(("parallel",)),
        name="peer_route",
    )(q, kb)


def _peer_dense_kernel(xn_ref, u_ref, v_ref, cnt_ref, e1_ref, r2_ref, e2_ref, res_ref,
                       out_ref, r_ref, *, n_heads, rows_per_step):
    e = pl.program_id(1)
    tb = xn_ref.shape[0]
    slab = 2 * SUBLANE

    @pl.when(e == 0)
    def _():
        out_ref[...] = res_ref[...]

    def routing_row(r, carry):
        i1 = e * rows_per_step + r
        bcast = lambda ref, h: jnp.broadcast_to(ref[h, pl.ds(i1, 1), :], (slab, tb)).astype(BF16)
        cnt_rows = [bcast(cnt_ref, h) for h in range(n_heads)]
        e1_rows = [bcast(e1_ref, h) for h in range(n_heads)]
        zero = jnp.zeros((slab, tb), BF16)
        for b in range(N_KEYS // slab):
            rs = slice(b * slab, (b + 1) * slab)
            acc = None
            for h in range(n_heads):
                sel = jnp.where(r2_ref[h, rs, :] < cnt_rows[h], e2_ref[h, rs, :], zero)
                term = sel * e1_rows[h]
                acc = term if acc is None else acc + term
            r_ref[pl.ds(pl.multiple_of(r * N_KEYS + b * slab, slab), slab), :] = acc
        return carry

    lax.fori_loop(0, rows_per_step, routing_row, 0)
    act = _dot_nt(u_ref[...], xn_ref[...])
    wt = _gelu(act).astype(BF16) * r_ref[...]
    out_ref[...] += _dot_tn(wt, v_ref[...])


def _peer_dense(xn, u, v, thr, e1, s2, e2, resid, *, tb=512, te=1024):
    N, D = xn.shape
    E = u.shape[0]
    n_heads = thr.shape[0]
    tb = min(tb, N)
    assert N % tb == 0 and E % te == 0 and te % N_KEYS == 0
    rps = te // N_KEYS
    once = pl.Buffered(1)
    tok = lambda: pl.BlockSpec((n_heads, N_KEYS, tb), lambda i, e: (0, 0, i), pipeline_mode=once)
    return pl.pallas_call(
        functools.partial(_peer_dense_kernel, n_heads=n_heads, rows_per_step=rps),
        grid=(N // tb, E // te),
        in_specs=[
            pl.BlockSpec((tb, D), lambda i, e: (i, 0), pipeline_mode=once),
            pl.BlockSpec((te, D), lambda i, e: (e, 0)),
            pl.BlockSpec((te, D), lambda i, e: (e, 0)),
            tok(), tok(), tok(), tok(),
            pl.BlockSpec((tb, D), lambda i, e: (i, 0), pipeline_mode=once),
        ],
        out_specs=pl.BlockSpec((tb, D), lambda i, e: (i, 0)),
        out_shape=jax.ShapeDtypeStruct((N, D), F32),
        scratch_shapes=[pltpu.VMEM((te, tb), BF16)],
        compiler_params=_cparams(("parallel", "arbitrary")),
        name="peer_dense",
    )(xn, u, v, thr, e1, s2, e2, resid)


def _peer_weights(w_q, sub_keys):
    D = w_q.shape[0]
    H = sub_keys.shape[0]
    half = sub_keys.shape[-1]
    wq = w_q.reshape(D, H, 2, half).transpose(0, 2, 1, 3).reshape(D, 2 * H * half).astype(BF16)
    eye = jnp.eye(H, dtype=sub_keys.dtype)
    sk = sub_keys.transpose(1, 0, 2, 3)
    bd = sk[:, :, :, None, :] * eye[None, :, None, :, None]
    hk = bd.reshape(2, H * N_KEYS, H * half)
    kh = bd.transpose(0, 2, 1, 3, 4).reshape(2, N_KEYS * H, H * half)
    return wq, jnp.concatenate([kh, hk], axis=1).astype(BF16)


def _peer(x, gain, wq, kb, u, v, *, n_heads):
    xn = _rms_norm(x, gain)
    q = _mm(xn, wq, out_dtype=BF16)
    thr, e1, s2, e2 = _peer_route(q, kb, n_heads=n_heads)
    return _peer_dense(xn, u, v, thr, e1, s2, e2, x)


def _pad_lanes(a, width=LANE):
    return jnp.pad(a, [(0, 0)] * (a.ndim - 1) + [(0, width - a.shape[-1])])


def kernel(x_prompt, x_sample, state_conv, state_h, cache_k, cache_v, cache_logf, page_table,
           norm_mix, norm_ffn,
           rec_w_in, rec_conv_w, rec_conv_b, rec_w_a, rec_b_a, rec_w_i, rec_b_i, rec_lambda, rec_w_out,
           att_w_in, att_b_f, att_q_norm, att_k_norm, att_w_out,
           peer_w_q, peer_sub_keys, peer_u, peer_v):
    Bp, S, D = x_prompt.shape
    Bd, Sd, _ = x_sample.shape
    assert Bp == 1
    H = D // HEAD_DIM
    PH = peer_sub_keys.shape[1]
    depth = norm_mix.shape[0]
    n_att, n_pool = cache_k.shape[0], cache_k.shape[1]
    page_w = PAGE_SIZE * H

    xp = x_prompt.reshape(S, D)
    xs = x_sample.reshape(Bd * Sd, D)
    row = lambda a: a.reshape(1, -1).astype(F32)
    cache_k2 = cache_k.reshape(n_att * n_pool, page_w, HEAD_DIM)
    cache_v2 = cache_v.reshape(n_att * n_pool, page_w, HEAD_DIM)
    cache_lf2 = cache_logf.reshape(n_att * n_pool, 1, page_w)

    pc, ph, pk, pv, plf = [], [], [], [], []
    sc, sh, sk, sv, slf = [], [], [], [], []
    for i in range(depth):
        j = i // 2
        if i % 2 == 0:
            w_in = _to_bf16(rec_w_in, j)
            w_out = _to_bf16(rec_w_out, j)
            wa = rec_w_a[j].astype(BF16)
            wi = rec_w_i[j].astype(BF16)
            cw, cb = rec_conv_w[j], row(rec_conv_b[j])
            ba, bi, lam = row(rec_b_a[j]), row(rec_b_i[j]), row(rec_lambda[j])
            proj_p = _mm(xp, w_in, gain=norm_mix[i])
            yp, hl_p = _rglru_prompt(proj_p, jnp.zeros((SUBLANE, D), F32), jnp.zeros((1, D), F32),
                                     cw, cb, wa, wi, ba, bi, lam)
            pc.append(proj_p[S - (CONV_W - 1):, D:].reshape(1, CONV_W - 1, D))
            ph.append(hl_p)
            xp = _mm(yp, w_out, residual=xp)
            proj_s = _mm(xs, w_in, gain=norm_mix[i]).reshape(Bd, Sd, 2 * D)
            ext = jnp.concatenate([state_conv[j], proj_s[..., D:]], axis=1)
            ys_tm, hl_s = _rglru_sample(proj_s[..., :D].transpose(1, 0, 2), ext.transpose(1, 0, 2),
                                        state_h[j], cw, cb, wa, wi, ba, bi, lam)
            sc.append(ext[:, Sd:])
            sh.append(hl_s)
            xs = _mm(ys_tm.transpose(1, 0, 2).reshape(Bd * Sd, D), w_out, residual=xs)
        else:
            w_qkv = att_w_in[j][:, :3 * D].astype(BF16)
            w_f = _pad_lanes(att_w_in[j][:, 3 * D:]).astype(BF16)
            w_out = _to_bf16(att_w_out, j)
            b_f = _pad_lanes(row(att_b_f[j]))
            qg, kg = row(att_q_norm[j]), row(att_k_norm[j])
            proj = _mm(xp, w_qkv, gain=norm_mix[i])
            projf = _mm(xp, w_f, gain=norm_mix[i])
            qb, kf, kbf, vt, lf, bias = _qkv_post(proj, projf, b_f, qg, kg, prompt=True)
            op = _flash(qb, kbf, vt, bias)
            pk.append(kf.reshape(1, S, H, HEAD_DIM))
            pv.append(proj[:, 2 * D:].reshape(1, S, H, HEAD_DIM))
            plf.append(lf[:, :H].reshape(1, S, H))
            xp = _mm(op, w_out, residual=xp)
            n_s = Bd * Sd
            proj = _mm(xs, w_qkv, gain=norm_mix[i])
            projf = _mm(xs, w_f, gain=norm_mix[i])
            qb, kf, kbf, vbf, lf = _qkv_post(proj, projf, b_f, qg, kg, prompt=False)
            lf_s = lf[:, :H].reshape(Bd, Sd, H)
            as_page = lambda a: jnp.pad(a.reshape(Bd, Sd * H, -1),
                                        ((0, 0), (0, page_w - Sd * H), (0, 0)))
            o = _decode(page_table + j * n_pool, qb.reshape(Bd, Sd * H, HEAD_DIM),
                        cache_k2, cache_v2, cache_lf2,
                        as_page(kbf), as_page(vbf), as_page(lf_s).reshape(Bd, 1, page_w))
            sk.append(kf.reshape(Bd, Sd, H, HEAD_DIM))
            sv.append(proj[:, 2 * D:].reshape(Bd, Sd, H, HEAD_DIM))
            slf.append(lf_s)
            xs = _mm(o.reshape(n_s, D), w_out, residual=xs)
        wq, kb = _peer_weights(peer_w_q[i], peer_sub_keys[i])
        u = _to_bf16(peer_u, i)
        v = _to_bf16(peer_v, i)
        xp = _peer(xp, norm_ffn[i], wq, kb, u, v, n_heads=PH)
        xs = _peer(xs, norm_ffn[i], wq, kb, u, v, n_heads=PH)

    return (xp.reshape(1, S, D), xs.reshape(Bd, Sd, D),
            jnp.stack(pc), jnp.stack(ph), jnp.stack(pk), jnp.stack(pv), jnp.stack(plf),
            jnp.stack(sc), jnp.stack(sh), jnp.stack(sk), jnp.stack(sv), jnp.stack(slf))
```

```python
import functools
import math

import jax
import jax.numpy as jnp
from jax import lax
from jax.experimental import pallas as pl
from jax.experimental.pallas import tpu as pltpu

F32 = jnp.float32
BF16 = jnp.bfloat16

EPS = 1e-6
NEG_INF = -1e30
BIG = 3.0e38
LOG2E = math.log2(math.e)
RG_C = 8.0
CONV_W = 4
HEAD_DIM = 128
PAGE_SIZE = 128
TOPK = 16
N_KEYS = 128
LANE = 128
SUBLANE = 8
VMEM_LIMIT = 56 * 1024 * 1024


def _cparams(sem, flags=None):
    return pltpu.CompilerParams(dimension_semantics=sem, vmem_limit_bytes=VMEM_LIMIT, flags=flags)


def _largest_tile(n, target, unit):
    t = min(target, n) // unit * unit
    while n % t:
        t -= unit
    return t


def _gelu(x):
    c = math.sqrt(2.0 / math.pi)
    return 0.5 * x * (1.0 + jnp.tanh(x * (c + (0.044715 * c) * (x * x))))


def _log_sigmoid(z):
    return jnp.minimum(z, 0.0) - jnp.log1p(jnp.exp(-jnp.abs(z)))


def _sigmoid(z):
    return 1.0 / (1.0 + jnp.exp(-z))


def _dot_nt(a, b):
    return lax.dot_general(a, b, (((1,), (1,)), ((), ())), preferred_element_type=F32)


def _dot_tn(a, b):
    return lax.dot_general(a, b, (((0,), (0,)), ((), ())), preferred_element_type=F32)


def _mm_kernel(*refs, norm, residual):
    it = iter(refs)
    x_ref = next(it)
    g_ref = next(it) if norm else None
    w_ref = next(it)
    res_ref = next(it) if residual else None
    out_ref = next(it)
    xn_ref = next(it) if norm else None

    if norm:
        @pl.when(pl.program_id(1) == 0)
        def _():
            xf = x_ref[...].astype(F32)
            y = xf * lax.rsqrt(jnp.mean(xf * xf, axis=-1, keepdims=True) + EPS)
            xn_ref[...] = (y * g_ref[...]).astype(BF16)
        lhs = xn_ref[...]
    else:
        lhs = x_ref[...].astype(BF16)
    acc = jnp.dot(lhs, w_ref[...], preferred_element_type=F32)
    if residual:
        acc = acc + res_ref[...]
    out_ref[...] = acc.astype(out_ref.dtype)


def _mm(x, w, *, gain=None, residual=None, out_dtype=F32, emit_xn=False, tm=1024, tn=1024):
    M, K = x.shape
    N = w.shape[1]
    tm = _largest_tile(M, tm, SUBLANE)
    tn = _largest_tile(N, tn, LANE)
    norm = gain is not None
    assert norm or not emit_xn
    in_specs = [pl.BlockSpec((tm, K), lambda i, j: (i, 0))]
    args = [x]
    if norm:
        in_specs.append(pl.BlockSpec((1, K), lambda i, j: (0, 0)))
        args.append(gain.reshape(1, K).astype(F32))
    in_specs.append(pl.BlockSpec((K, tn), lambda i, j: (0, j)))
    args.append(w)
    if residual is not None:
        in_specs.append(pl.BlockSpec((tm, tn), lambda i, j: (i, j)))
        args.append(residual)
    out_specs = [pl.BlockSpec((tm, tn), lambda i, j: (i, j))]
    out_shape = [jax.ShapeDtypeStruct((M, N), out_dtype)]
    scratch = []
    if emit_xn:
        out_specs.append(pl.BlockSpec((tm, K), lambda i, j: (i, 0)))
        out_shape.append(jax.ShapeDtypeStruct((M, K), BF16))
    elif norm:
        scratch = [pltpu.VMEM((tm, K), BF16)]
    outs = pl.pallas_call(
        functools.partial(_mm_kernel, norm=norm, residual=residual is not None),
        grid=(M // tm, N // tn),
        in_specs=in_specs,
        out_specs=out_specs,
        out_shape=out_shape,
        scratch_shapes=scratch,
        compiler_params=_cparams(("parallel", "arbitrary")),
        name="mm",
    )(*args)
    return tuple(outs) if emit_xn else outs[0]


def _cast_kernel(x_ref, o_ref):
    o_ref[...] = x_ref[...].astype(o_ref.dtype)


def _to_bf16(w, layer, *, rows=1024):
    _, M, K = w.shape
    rows = min(rows, M)
    assert M % rows == 0
    return pl.pallas_call(
        _cast_kernel,
        grid=(M // rows,),
        in_specs=[pl.BlockSpec((None, rows, K), lambda i: (layer, i, 0))],
        out_specs=pl.BlockSpec((rows, K), lambda i: (i, 0)),
        out_shape=jax.ShapeDtypeStruct((M, K), BF16),
        compiler_params=_cparams(("parallel",)),
        name="to_bf16",
    )(w)


def _rg_coeffs(xc, w_a, w_i, b_a, b_i, lam):
    xcb = xc.astype(BF16)
    r = _sigmoid(jnp.dot(xcb, w_a, preferred_element_type=F32) + b_a)
    i = _sigmoid(jnp.dot(xcb, w_i, preferred_element_type=F32) + b_i)
    log_a = RG_C * r * _log_sigmoid(lam)
    a = jnp.exp(log_a)
    th = jnp.tanh(log_a)
    b = jnp.sqrt(-2.0 * th / (1.0 - th)) * (i * xc)
    return a, b


def _scan_rows(a, b):
    n = a.shape[0]
    row = lax.broadcasted_iota(jnp.int32, a.shape, 0)
    s = 1
    while s < n:
        a_sh = pltpu.roll(a, s, axis=0)
        b_sh = pltpu.roll(b, s, axis=0)
        valid = row >= s
        b = jnp.where(valid, a * b_sh + b, b)
        a = jnp.where(valid, a * a_sh, a)
        s *= 2
    return a, b


def _cumsum_rows(x):
    n = x.shape[0]
    row = lax.broadcasted_iota(jnp.int32, x.shape, 0)
    s = 1
    while s < n:
        x = x + jnp.where(row >= s, pltpu.roll(x, s, axis=0), 0.0)
        s *= 2
    return x


def _rglru_prompt_kernel(gate_ref, xr_ref, cw_ref, cb_ref, wa_ref, wi_ref, ba_ref, bi_ref,
                         lam_ref, init_ref, h0_ref, y_ref, hl_ref, ext_ref, h_ref, *, tt):
    t = pl.program_id(1)

    @pl.when(t == 0)
    def _():
        ext_ref[0:SUBLANE, :] = init_ref[...]
        h_ref[...] = h0_ref[...]

    xr = xr_ref[...]
    ext_ref[SUBLANE:SUBLANE + tt, :] = xr
    xc = cb_ref[...] + xr * cw_ref[CONV_W - 1:CONV_W, :]
    for k in range(CONV_W - 1):
        xc = xc + ext_ref[pl.ds(SUBLANE - (CONV_W - 1) + k, tt), :] * cw_ref[k:k + 1, :]
    ext_ref[0:SUBLANE, :] = xr[tt - SUBLANE:tt, :]

    a, b = _rg_coeffs(xc, wa_ref[...], wi_ref[...], ba_ref[...], bi_ref[...], lam_ref[...])
    A, B = _scan_rows(a, b)
    h = A * h_ref[...] + B
    h_ref[...] = h[tt - 1:tt, :]
    hl_ref[...] = h[tt - 1:tt, :]
    y_ref[...] = (h * _gelu(gate_ref[...])).astype(y_ref.dtype)


def _rglru_prompt(proj, conv_init, h0, cw, cb, wa, wi, ba, bi, lam, *, tt=256):
    T = proj.shape[0]
    D = proj.shape[1] // 2
    nb, C = wa.shape[0], wa.shape[1]
    tt = min(tt, T)
    assert T % tt == 0 and tt % SUBLANE == 0
    vec = lambda: pl.BlockSpec((1, C), lambda n, t: (0, n))
    y, hl = pl.pallas_call(
        functools.partial(_rglru_prompt_kernel, tt=tt),
        grid=(nb, T // tt),
        in_specs=[
            pl.BlockSpec((tt, C), lambda n, t: (t, n)),
            pl.BlockSpec((tt, C), lambda n, t: (t, nb + n)),
            pl.BlockSpec((CONV_W, C), lambda n, t: (0, n)),
            vec(),
            pl.BlockSpec((None, C, C), lambda n, t: (n, 0, 0)),
            pl.BlockSpec((None, C, C), lambda n, t: (n, 0, 0)),
            vec(), vec(), vec(),
            pl.BlockSpec((SUBLANE, C), lambda n, t: (0, n)),
            vec(),
        ],
        out_specs=[pl.BlockSpec((tt, C), lambda n, t: (t, n)),
                   pl.BlockSpec((1, C), lambda n, t: (0, n))],
        out_shape=[jax.ShapeDtypeStruct((T, D), BF16), jax.ShapeDtypeStruct((1, D), F32)],
        scratch_shapes=[pltpu.VMEM((tt + SUBLANE, C), F32), pltpu.VMEM((1, C), F32)],
        compiler_params=_cparams(("parallel", "arbitrary")),
        name="rglru_prompt",
    )(proj, proj, cw, cb, wa, wi, ba, bi, lam, conv_init, h0)
    return y, hl


def _rglru_sample_kernel(gate_ref, ext_ref, cw_ref, cb_ref, wa_ref, wi_ref, ba_ref, bi_ref,
                         lam_ref, h0_ref, y_ref, hl_ref, *, steps):
    h = h0_ref[...]
    for t in range(steps):
        xc = cb_ref[...]
        for k in range(CONV_W):
            xc = xc + ext_ref[t + k] * cw_ref[k:k + 1, :]
        a, b = _rg_coeffs(xc, wa_ref[...], wi_ref[...], ba_ref[...], bi_ref[...], lam_ref[...])
        h = a * h + b
        y_ref[t] = (h * _gelu(gate_ref[t])).astype(y_ref.dtype)
    hl_ref[...] = h


def _rglru_sample(gate_tm, ext_tm, h0, cw, cb, wa, wi, ba, bi, lam):
    S, B, D = gate_tm.shape
    nb, C = wa.shape[0], wa.shape[1]
    vec = lambda: pl.BlockSpec((1, C), lambda n: (0, n))
    return pl.pallas_call(
        functools.partial(_rglru_sample_kernel, steps=S),
        grid=(nb,),
        in_specs=[
            pl.BlockSpec((S, B, C), lambda n: (0, 0, n)),
            pl.BlockSpec((S + CONV_W - 1, B, C), lambda n: (0, 0, n)),
            pl.BlockSpec((CONV_W, C), lambda n: (0, n)),
            vec(),
            pl.BlockSpec((None, C, C), lambda n: (n, 0, 0)),
            pl.BlockSpec((None, C, C), lambda n: (n, 0, 0)),
            vec(), vec(), vec(),
            pl.BlockSpec((B, C), lambda n: (0, n)),
        ],
        out_specs=[pl.BlockSpec((S, B, C), lambda n: (0, 0, n)),
                   pl.BlockSpec((B, C), lambda n: (0, n))],
        out_shape=[jax.ShapeDtypeStruct((S, B, D), BF16), jax.ShapeDtypeStruct((B, D), F32)],
        compiler_params=_cparams(("parallel",)),
        name="rglru_sample",
    )(gate_tm, ext_tm, cw, cb, wa, wi, ba, bi, lam, h0)


def _qkv_post_kernel(q_ref, k_ref, v_ref, f_ref, bf_ref, qg_ref, kg_ref, *refs, prompt):
    if prompt:
        qb_ref, kf_ref, kb_ref, vt_ref, lf_ref, bias_ref, vf_ref, carry_ref = refs
        vf_ref[...] = v_ref[...]
    else:
        qb_ref, kf_ref, kb_ref, vb_ref, lf_ref = refs
    n_heads = q_ref.shape[1] // HEAD_DIM
    q_scale = HEAD_DIM ** -0.5 * LOG2E
    for h in range(n_heads):
        sl = slice(h * HEAD_DIM, (h + 1) * HEAD_DIM)
        qh = q_ref[:, sl]
        qn = qh * lax.rsqrt(jnp.mean(qh * qh, axis=-1, keepdims=True) + EPS) * qg_ref[...]
        qb_ref[:, sl] = (qn * q_scale).astype(BF16)
        kh = k_ref[:, sl]
        kn = kh * lax.rsqrt(jnp.mean(kh * kh, axis=-1, keepdims=True) + EPS) * kg_ref[...]
        kf_ref[:, sl] = kn
        kb_ref[:, sl] = kn.astype(BF16)
        if prompt:
            vt_ref[sl, :] = v_ref[:, sl].T.astype(BF16)
    if not prompt:
        vb_ref[...] = v_ref[...].astype(BF16)
    lf = _log_sigmoid(f_ref[...] + bf_ref[...])
    lf_ref[...] = lf
    if prompt:
        @pl.when(pl.program_id(0) == 0)
        def _():
            carry_ref[...] = jnp.zeros_like(carry_ref)

        c = _cumsum_rows(lf) + carry_ref[...]
        carry_ref[...] = c[c.shape[0] - 1:, :]
        nb = c * (-LOG2E)
        for h in range(n_heads):
            bias_ref[h] = jnp.broadcast_to(nb[:, h:h + 1], (nb.shape[0], LANE))


def _qkv_post(proj, projf, b_f, q_g, k_g, *, prompt, tm=256):
    T = proj.shape[0]
    D = proj.shape[1] // 3
    H = D // HEAD_DIM
    tm = min(tm, T)
    assert T % tm == 0
    blk = lambda j: pl.BlockSpec((tm, D), lambda i: (i, j))
    small = pl.BlockSpec((tm, LANE), lambda i: (i, 0))
    vec = pl.BlockSpec((1, LANE), lambda i: (0, 0))
    out_specs = [blk(0), blk(0), blk(0)]
    out_shape = [jax.ShapeDtypeStruct((T, D), BF16), jax.ShapeDtypeStruct((T, D), F32),
                 jax.ShapeDtypeStruct((T, D), BF16)]
    if prompt:
        out_specs += [pl.BlockSpec((D, tm), lambda i: (0, i)), small,
                      pl.BlockSpec((H, tm, LANE), lambda i: (0, i, 0)), blk(0)]
        out_shape += [jax.ShapeDtypeStruct((D, T), BF16), jax.ShapeDtypeStruct((T, LANE), F32),
                      jax.ShapeDtypeStruct((H, T, LANE), F32), jax.ShapeDtypeStruct((T, D), F32)]
        scratch = [pltpu.VMEM((1, LANE), F32)]
    else:
        out_specs += [blk(0), small]
        out_shape += [jax.ShapeDtypeStruct((T, D), BF16), jax.ShapeDtypeStruct((T, LANE), F32)]
        scratch = []
    return pl.pallas_call(
        functools.partial(_qkv_post_kernel, prompt=prompt),
        grid=(T // tm,),
        in_specs=[blk(0), blk(1), blk(2), small, vec, vec, vec],
        out_specs=out_specs,
        out_shape=out_shape,
        scratch_shapes=scratch,
        compiler_params=_cparams(("arbitrary",)),
        name="qkv_post",
    )(proj, proj, proj, projf, b_f, q_g, k_g)


def _flash_kernel(q_ref, k_ref, vt_ref, b_ref, o_ref, s_ref, m_ref, l_ref, acc_ref, *, tq, hps):
    qi = pl.program_id(1)
    m_ref[...] = jnp.full_like(m_ref, NEG_INF)
    l_ref[...] = jnp.zeros_like(l_ref)
    acc_ref[...] = jnp.zeros_like(acc_ref)
    heads = [slice(g * HEAD_DIM, (g + 1) * HEAD_DIM) for g in range(hps)]

    def scores(g, i):
        k0 = pl.multiple_of(i * tq, tq)
        bias = b_ref[g, pl.ds(k0, tq), :]
        return (_dot_nt(k_ref[pl.ds(k0, tq), heads[g]], q_ref[:, heads[g]])
                + jnp.concatenate([bias] * (tq // LANE), axis=1))

    def update(g, s, i):
        k0 = pl.multiple_of(i * tq, tq)
        m_prev = m_ref[g]
        m_new = jnp.maximum(m_prev, jnp.max(s, axis=0, keepdims=True))
        alpha = jnp.exp2(m_prev - m_new)
        p = jnp.exp2(s - m_new)
        l_ref[g] = alpha * l_ref[g] + jnp.sum(p, axis=0, keepdims=True)
        acc_ref[g] = alpha * acc_ref[g] + jnp.dot(vt_ref[heads[g], pl.ds(k0, tq)], p.astype(BF16),
                                                 preferred_element_type=F32)
        m_ref[g] = m_new

    for g in range(hps):
        s_ref[g, 0] = scores(g, 0)

    def body(i, carry):
        nxt = [scores(g, i + 1) for g in range(hps)]
        for g in range(hps):
            update(g, s_ref[g, i % 2], i)
        for g in range(hps):
            s_ref[g, (i + 1) % 2] = nxt[g]
        return carry

    lax.fori_loop(0, qi, body, 0)
    for g in range(hps):
        s = s_ref[g, qi % 2]
        key = lax.broadcasted_iota(jnp.int32, s.shape, 0)
        qry = lax.broadcasted_iota(jnp.int32, s.shape, 1)
        update(g, jnp.where(key <= qry, s, NEG_INF), qi)
        o_ref[:, heads[g]] = (acc_ref[g] / l_ref[g]).T.astype(o_ref.dtype)


def _flash(q, k, vt, bias, *, tq=1024, hps=1):
    T, D = q.shape
    H = D // HEAD_DIM
    tq = min(tq, T)
    assert T % tq == 0 and H % hps == 0
    W = hps * HEAD_DIM
    return pl.pallas_call(
        functools.partial(_flash_kernel, tq=tq, hps=hps),
        grid=(H // hps, T // tq),
        in_specs=[
            pl.BlockSpec((tq, W), lambda h, i: (i, h)),
            pl.BlockSpec((T, W), lambda h, i: (0, h)),
            pl.BlockSpec((W, T), lambda h, i: (h, 0)),
            pl.BlockSpec((hps, T, LANE), lambda h, i: (h, 0, 0)),
        ],
        out_specs=pl.BlockSpec((tq, W), lambda h, i: (i, h)),
        out_shape=jax.ShapeDtypeStruct((T, D), BF16),
        scratch_shapes=[pltpu.VMEM((hps, 2, tq, tq), F32), pltpu.VMEM((hps, 1, tq), F32),
                        pltpu.VMEM((hps, 1, tq), F32), pltpu.VMEM((hps, HEAD_DIM, tq), F32)],
        compiler_params=_cparams(("parallel", "arbitrary")),
        name="flash",
    )(q, k, vt, bias)


def _decode_kernel(pt_ref, q_ref, *refs, pps, n_steps, n_heads):
    kc = refs[0:pps]
    vc = refs[pps:2 * pps]
    lc = refs[2 * pps:3 * pps]
    kn_ref, vn_ref, ln_ref, o_ref, m_ref, l_ref, acc_ref, carry_ref, hm_ref = refs[3 * pps:]
    p = pl.program_id(1)
    width = hm_ref.shape[1]

    @pl.when(p == 0)
    def _():
        m_ref[...] = jnp.full_like(m_ref, NEG_INF)
        l_ref[...] = jnp.zeros_like(l_ref)
        acc_ref[...] = jnp.zeros_like(acc_ref)
        carry_ref[...] = jnp.zeros_like(carry_ref)
        row = lax.broadcasted_iota(jnp.int32, hm_ref.shape, 0)
        col = lax.broadcasted_iota(jnp.int32, hm_ref.shape, 1)
        hm_ref[...] = jnp.where((col % n_heads) == (row % n_heads), 0.0, NEG_INF)

    def page_bias(lf_ref):
        x = jnp.broadcast_to(lf_ref[...], carry_ref.shape)
        lane = lax.broadcasted_iota(jnp.int32, x.shape, 1)
        tot = x
        sh = n_heads
        while sh < width:
            x = x + jnp.where(lane >= sh, pltpu.roll(x, sh, axis=1), 0.0)
            tot = tot + pltpu.roll(tot, sh, axis=1)
            sh *= 2
        c = x + carry_ref[...]
        carry_ref[...] = carry_ref[...] + tot
        return c[0:1, :] * (-LOG2E)

    def attend(ks, vs, biases, causal):
        q = q_ref[...]
        ss = []
        for kb, b in zip(ks, biases):
            s = _dot_nt(q, kb) + (hm_ref[...] + b)
            if causal:
                row = lax.broadcasted_iota(jnp.int32, s.shape, 0)
                col = lax.broadcasted_iota(jnp.int32, s.shape, 1)
                s = jnp.where(col // n_heads <= row // n_heads, s, NEG_INF)
            ss.append(s)
        m_prev = m_ref[...]
        m_new = m_prev
        for s in ss:
            m_new = jnp.maximum(m_new, jnp.max(s, axis=-1, keepdims=True))
        alpha = jnp.exp2(m_prev - m_new)
        l = alpha * l_ref[...]
        acc = alpha * acc_ref[...]
        for s, vb in zip(ss, vs):
            pr = jnp.exp2(s - m_new)
            l = l + jnp.sum(pr, axis=-1, keepdims=True)
            acc = acc + jnp.dot(pr.astype(BF16), vb, preferred_element_type=F32)
        m_ref[...] = m_new
        l_ref[...] = l
        acc_ref[...] = acc

    @pl.when(p < n_steps)
    def _():
        biases = [page_bias(r) for r in lc]
        attend([r[...].astype(BF16) for r in kc], [r[...].astype(BF16) for r in vc], biases, False)

    @pl.when(p == n_steps)
    def _():
        attend([kn_ref[...]], [vn_ref[...]], [page_bias(ln_ref)], True)
        o_ref[...] = acc_ref[...] / l_ref[...]


def _decode(page_ids, q2, cache_k, cache_v, cache_lf, k_new, v_new, lf_new, *, pps=8):
    Bd, P = page_ids.shape
    R = q2.shape[1]
    W = cache_k.shape[1]
    H = W // PAGE_SIZE
    assert P % pps == 0
    n_steps = P // pps
    last = n_steps - 1

    def page_map(i):
        return lambda b, p, pt: (pt[b, jnp.minimum(p, last) * pps + i], 0, 0)

    seq_map = lambda b, p, pt: (b, 0, 0)
    in_specs = [pl.BlockSpec((None, R, HEAD_DIM), seq_map)]
    in_specs += [pl.BlockSpec((None, W, HEAD_DIM), page_map(i)) for i in range(pps)]
    in_specs += [pl.BlockSpec((None, W, HEAD_DIM), page_map(i)) for i in range(pps)]
    in_specs += [pl.BlockSpec((None, 1, W), page_map(i)) for i in range(pps)]
    in_specs += [pl.BlockSpec((None, W, HEAD_DIM), seq_map), pl.BlockSpec((None, W, HEAD_DIM), seq_map),
                 pl.BlockSpec((None, 1, W), seq_map)]
    grid_spec = pltpu.PrefetchScalarGridSpec(
        num_scalar_prefetch=1,
        grid=(Bd, n_steps + 1),
        in_specs=in_specs,
        out_specs=pl.BlockSpec((None, R, HEAD_DIM), seq_map),
        scratch_shapes=[pltpu.VMEM((R, 1), F32), pltpu.VMEM((R, 1), F32), pltpu.VMEM((R, HEAD_DIM), F32),
                        pltpu.VMEM((SUBLANE, W), F32), pltpu.VMEM((R, W), F32)],
    )
    return pl.pallas_call(
        functools.partial(_decode_kernel, pps=pps, n_steps=n_steps, n_heads=H),
        grid_spec=grid_spec,
        out_shape=jax.ShapeDtypeStruct((Bd, R, HEAD_DIM), F32),
        compiler_params=_cparams(("parallel", "arbitrary")),
        name="decode",
    )(page_ids, q2, *([cache_k] * pps), *([cache_v] * pps), *([cache_lf] * pps), k_new, v_new, lf_new)


def _cmpx(v, i, j):
    hi = jnp.maximum(v[i], v[j])
    lo = jnp.minimum(v[i], v[j])
    v[i], v[j] = hi, lo


def _bitonic_merge_desc(v):
    n = len(v)
    j = n // 2
    while j >= 1:
        for i in range(n):
            if i & j == 0:
                _cmpx(v, i, i | j)
        j //= 2
    return v


def _sort_desc(v):
    v = list(v)
    n = len(v)
    k = 2
    while k <= n:
        j = k // 2
        while j >= 1:
            for i in range(n):
                l = i ^ j
                if l > i:
                    if (i & k) == 0:
                        _cmpx(v, i, l)
                    else:
                        _cmpx(v, l, i)
            j //= 2
        k *= 2
    return v


def _top16_sorted(v):
    groups = [_sort_desc(v[g:g + TOPK]) for g in range(0, len(v), TOPK)]
    while len(groups) > 1:
        nxt = []
        for g in range(0, len(groups), 2):
            a, b = groups[g], groups[g + 1]
            c = [jnp.maximum(a[i], b[TOPK - 1 - i]) for i in range(TOPK)]
            nxt.append(_bitonic_merge_desc(c))
        groups = nxt
    return groups[0]


def _peer_route_kernel(q_ref, kb_ref, cnt_ref, e1_ref, r2_ref, e2_ref, *, n_heads):
    nrow = n_heads * N_KEYS
    tb = q_ref.shape[0]
    tops = []
    per_head = []
    for p in range(2):
        qp = q_ref[:, p * nrow:(p + 1) * nrow]
        res = _dot_nt(kb_ref[p], qp)
        srt = res[0:nrow].reshape(N_KEYS, n_heads, tb)
        per_head.append(res[nrow:2 * nrow].reshape(n_heads, N_KEYS, tb))
        tops.append(_top16_sorted([srt[k] for k in range(N_KEYS)]))
    v1, v2 = tops
    pairs = [(a, b) for a in range(TOPK) for b in range(TOPK) if (a + 1) * (b + 1) <= TOPK]
    cand = {ab: v1[ab[0]] + v2[ab[1]] for ab in pairs}
    pad = jnp.full_like(v1[0], -BIG)
    top = _top16_sorted([cand[ab] for ab in pairs] + [pad] * (4 * TOPK - len(pairs)))
    tau = top[TOPK - 1]
    z = jnp.ones_like(tau)
    for k in range(1, TOPK):
        z = z + jnp.exp(top[k] - top[0])
    inv_z = 1.0 / z
    cnt_rank = []
    for a in range(TOPK):
        n = jnp.zeros_like(tau)
        for b in range(TOPK):
            if (a, b) in cand:
                n = n + jnp.where(cand[(a, b)] >= tau, 1.0, 0.0)
        cnt_rank.append(n)
    s1, s2 = per_head
    for h in range(n_heads):
        row = lambda x: x[h:h + 1, :]
        s1h, s2h = s1[h], s2[h]
        cnt = jnp.zeros_like(s1h)
        r2 = jnp.zeros_like(s2h)
        for a in range(TOPK):
            cnt = jnp.maximum(cnt, jnp.where(s1h >= row(v1[a]), row(cnt_rank[a]), 0.0))
            r2 = r2 + jnp.where(row(v2[a]) > s2h, 1.0, 0.0)
        cnt_ref[h] = cnt
        r2_ref[h] = r2.astype(BF16)
        e1_ref[h] = jnp.exp(s1h - row(v1[0]))
        e2_ref[h] = (jnp.exp(s2h - row(v2[0])) * row(inv_z)).astype(BF16)


def _peer_route(q, kb, *, n_heads, tb=256):
    N = q.shape[0]
    tb = min(tb, N)
    assert N % tb == 0
    big = lambda: pl.BlockSpec((n_heads, N_KEYS, tb), lambda i: (0, 0, i))
    shp = lambda dt: jax.ShapeDtypeStruct((n_heads, N_KEYS, N), dt)
    return pl.pallas_call(
        functools.partial(_peer_route_kernel, n_heads=n_heads),
        grid=(N // tb,),
        in_specs=[pl.BlockSpec((tb, q.shape[1]), lambda i: (i, 0)),
                  pl.BlockSpec(kb.shape, lambda i: (0, 0, 0))],
        out_specs=[big(), big(), big(), big()],
        out_shape=[shp(F32), shp(F32), shp(BF16), shp(BF16)],
        compiler_params=_cparams(("parallel",)),
        name="peer_route",
    )(q, kb)


def _peer_dense_kernel(xn_ref, u_ref, v_ref, cnt_ref, e1_ref, r2_ref, e2_ref, res_ref,
                       out_ref, r_ref, *, n_heads, rows_per_step):
    e = pl.program_id(1)
    tb = xn_ref.shape[0]
    slab = 2 * SUBLANE

    @pl.when(e == 0)
    def _():
        out_ref[...] = res_ref[...]

    def routing_row(r, carry):
        i1 = e * rows_per_step + r
        bcast = lambda ref, h: jnp.broadcast_to(ref[h, pl.ds(i1, 1), :], (slab, tb)).astype(BF16)
        cnt_rows = [bcast(cnt_ref, h) for h in range(n_heads)]
        e1_rows = [bcast(e1_ref, h) for h in range(n_heads)]
        zero = jnp.zeros((slab, tb), BF16)
        for b in range(N_KEYS // slab):
            rs = slice(b * slab, (b + 1) * slab)
            acc = None
            for h in range(n_heads):
                sel = jnp.where(r2_ref[h, rs, :] < cnt_rows[h], e2_ref[h, rs, :], zero)
                term = sel * e1_rows[h]
                acc = term if acc is None else acc + term
            r_ref[pl.ds(pl.multiple_of(r * N_KEYS + b * slab, slab), slab), :] = acc
        return carry

    lax.fori_loop(0, rows_per_step, routing_row, 0)
    act = _dot_nt(u_ref[...], xn_ref[...])
    wt = _gelu(act).astype(BF16) * r_ref[...]
    out_ref[...] += _dot_tn(wt, v_ref[...])


def _peer_dense(xn, u, v, thr, e1, s2, e2, resid, *, tb=512, te=1024):
    N, D = xn.shape
    E = u.shape[0]
    n_heads = thr.shape[0]
    tb = min(tb, N)
    assert N % tb == 0 and E % te == 0 and te % N_KEYS == 0
    rps = te // N_KEYS
    once = pl.Buffered(1)
    tok = lambda: pl.BlockSpec((n_heads, N_KEYS, tb), lambda i, e: (0, 0, i), pipeline_mode=once)
    return pl.pallas_call(
        functools.partial(_peer_dense_kernel, n_heads=n_heads, rows_per_step=rps),
        grid=(N // tb, E // te),
        in_specs=[
            pl.BlockSpec((tb, D), lambda i, e: (i, 0), pipeline_mode=once),
            pl.BlockSpec((te, D), lambda i, e: (e, 0)),
            pl.BlockSpec((te, D), lambda i, e: (e, 0)),
            tok(), tok(), tok(), tok(),
            pl.BlockSpec((tb, D), lambda i, e: (i, 0), pipeline_mode=once),
        ],
        out_specs=pl.BlockSpec((tb, D), lambda i, e: (i, 0)),
        out_shape=jax.ShapeDtypeStruct((N, D), F32),
        scratch_shapes=[pltpu.VMEM((te, tb), BF16)],
        compiler_params=_cparams(("parallel", "arbitrary")),
        name="peer_dense",
    )(xn, u, v, thr, e1, s2, e2, resid)


def _peer_weights(w_q, sub_keys):
    D = w_q.shape[0]
    H = sub_keys.shape[0]
    half = sub_keys.shape[-1]
    wq = w_q.reshape(D, H, 2, half).transpose(0, 2, 1, 3).reshape(D, 2 * H * half).astype(BF16)
    eye = jnp.eye(H, dtype=sub_keys.dtype)
    sk = sub_keys.transpose(1, 0, 2, 3)
    bd = sk[:, :, :, None, :] * eye[None, :, None, :, None]
    hk = bd.reshape(2, H * N_KEYS, H * half)
    kh = bd.transpose(0, 2, 1, 3, 4).reshape(2, N_KEYS * H, H * half)
    return wq, jnp.concatenate([kh, hk], axis=1).astype(BF16)


def _peer(x, gain, wq, kb, u, v, *, n_heads):
    q, xn = _mm(x, wq, gain=gain, out_dtype=BF16, emit_xn=True)
    thr, e1, s2, e2 = _peer_route(q, kb, n_heads=n_heads)
    return _peer_dense(xn, u, v, thr, e1, s2, e2, x)


def _pad_lanes(a, width=LANE):
    return jnp.pad(a, [(0, 0)] * (a.ndim - 1) + [(0, width - a.shape[-1])])


def kernel(x_prompt, x_sample, state_conv, state_h, cache_k, cache_v, cache_logf, page_table,
           norm_mix, norm_ffn,
           rec_w_in, rec_conv_w, rec_conv_b, rec_w_a, rec_b_a, rec_w_i, rec_b_i, rec_lambda, rec_w_out,
           att_w_in, att_b_f, att_q_norm, att_k_norm, att_w_out,
           peer_w_q, peer_sub_keys, peer_u, peer_v):
    Bp, S, D = x_prompt.shape
    Bd, Sd, _ = x_sample.shape
    assert Bp == 1
    H = D // HEAD_DIM
    PH = peer_sub_keys.shape[1]
    depth = norm_mix.shape[0]
    n_att, n_pool = cache_k.shape[0], cache_k.shape[1]
    page_w = PAGE_SIZE * H

    xp = x_prompt.reshape(S, D)
    xs = x_sample.reshape(Bd * Sd, D)
    row = lambda a: a.reshape(1, -1).astype(F32)
    cache_k2 = cache_k.reshape(n_att * n_pool, page_w, HEAD_DIM)
    cache_v2 = cache_v.reshape(n_att * n_pool, page_w, HEAD_DIM)
    cache_lf2 = cache_logf.reshape(n_att * n_pool, 1, page_w)

    pc, ph, pk, pv, plf = [], [], [], [], []
    sc, sh, sk, sv, slf = [], [], [], [], []
    for i in range(depth):
        j = i // 2
        if i % 2 == 0:
            w_in = _to_bf16(rec_w_in, j)
            w_out = _to_bf16(rec_w_out, j)
            wa = rec_w_a[j].astype(BF16)
            wi = rec_w_i[j].astype(BF16)
            cw, cb = rec_conv_w[j], row(rec_conv_b[j])
            ba, bi, lam = row(rec_b_a[j]), row(rec_b_i[j]), row(rec_lambda[j])
            proj_p = _mm(xp, w_in, gain=norm_mix[i])
            yp, hl_p = _rglru_prompt(proj_p, jnp.zeros((SUBLANE, D), F32), jnp.zeros((1, D), F32),
                                     cw, cb, wa, wi, ba, bi, lam)
            pc.append(proj_p[S - (CONV_W - 1):, D:].reshape(1, CONV_W - 1, D))
            ph.append(hl_p)
            xp = _mm(yp, w_out, residual=xp)
            proj_s = _mm(xs, w_in, gain=norm_mix[i]).reshape(Bd, Sd, 2 * D)
            ext = jnp.concatenate([state_conv[j], proj_s[..., D:]], axis=1)
            ys_tm, hl_s = _rglru_sample(proj_s[..., :D].transpose(1, 0, 2), ext.transpose(1, 0, 2),
                                        state_h[j], cw, cb, wa, wi, ba, bi, lam)
            sc.append(ext[:, Sd:])
            sh.append(hl_s)
            xs = _mm(ys_tm.transpose(1, 0, 2).reshape(Bd * Sd, D), w_out, residual=xs)
        else:
            w_qkv = att_w_in[j][:, :3 * D].astype(BF16)
            w_f = _pad_lanes(att_w_in[j][:, 3 * D:]).astype(BF16)
            w_out = _to_bf16(att_w_out, j)
            b_f = _pad_lanes(row(att_b_f[j]))
            qg, kg = row(att_q_norm[j]), row(att_k_norm[j])
            proj = _mm(xp, w_qkv, gain=norm_mix[i])
            projf = _mm(xp, w_f, gain=norm_mix[i])
            qb, kf, kbf, vt, lf, bias, vf = _qkv_post(proj, projf, b_f, qg, kg, prompt=True)
            op = _flash(qb, kbf, vt, bias)
            pk.append(kf.reshape(1, S, H, HEAD_DIM))
            pv.append(vf.reshape(1, S, H, HEAD_DIM))
            plf.append(lf[:, :H].reshape(1, S, H))
            xp = _mm(op, w_out, residual=xp)
            n_s = Bd * Sd
            proj = _mm(xs, w_qkv, gain=norm_mix[i])
            projf = _mm(xs, w_f, gain=norm_mix[i])
            qb, kf, kbf, vbf, lf = _qkv_post(proj, projf, b_f, qg, kg, prompt=False)
            lf_s = lf[:, :H].reshape(Bd, Sd, H)
            as_page = lambda a: jnp.pad(a.reshape(Bd, Sd * H, -1),
                                        ((0, 0), (0, page_w - Sd * H), (0, 0)))
            o = _decode(page_table + j * n_pool, qb.reshape(Bd, Sd * H, HEAD_DIM),
                        cache_k2, cache_v2, cache_lf2,
                        as_page(kbf), as_page(vbf), as_page(lf_s).reshape(Bd, 1, page_w))
            sk.append(kf.reshape(Bd, Sd, H, HEAD_DIM))
            sv.append(proj[:, 2 * D:].reshape(Bd, Sd, H, HEAD_DIM))
            slf.append(lf_s)
            xs = _mm(o.reshape(n_s, D), w_out, residual=xs)
        wq, kb = _peer_weights(peer_w_q[i], peer_sub_keys[i])
        u = _to_bf16(peer_u, i)
        v = _to_bf16(peer_v, i)
        xp = _peer(xp, norm_ffn[i], wq, kb, u, v, n_heads=PH)
        xs = _peer(xs, norm_ffn[i], wq, kb, u, v, n_heads=PH)

    return (xp.reshape(1, S, D), xs.reshape(Bd, Sd, D),
            jnp.stack(pc), jnp.stack(ph), jnp.stack(pk), jnp.stack(pv), jnp.stack(plf),
            jnp.stack(sc), jnp.stack(sh), jnp.stack(sk), jnp.stack(sv), jnp.stack(slf))
```
